```python
import jax
import jax.numpy as jnp
from jax import lax
import numpy as np

D_MODEL = 1024
BATCH = 2
SEQ = 16384
DEPTH = 2

BRANCH_WIDTH = D_MODEL // 2
N_BRANCH = 3
ATTN_HEADS = 8
ATTN_HEAD_DIM = BRANCH_WIDTH // ATTN_HEADS
ATTN_WIDTH = ATTN_HEADS * ATTN_HEAD_DIM
MOBA_BLOCK = 256
MOBA_TOPK = 3
QUERY_CHUNK = 128
RWKV_HEAD_DIM = 64
RWKV_HEADS = BRANCH_WIDTH // RWKV_HEAD_DIM
RWKV_WIDTH = RWKV_HEADS * RWKV_HEAD_DIM
DECAY_LORA = 64
ICLR_LORA = 64
GATE_LORA = 128
VMIX_LORA = 32
LNX_EPS = 64e-5
POOL_WINDOWS = (2, 4, 8, 16)
POOL_GROUPS = 4
POOL_GROUP_DIM = BRANCH_WIDTH // POOL_GROUPS
POOL_WIDTH = POOL_GROUPS * POOL_GROUP_DIM
N_GROUPS = 4
EXPERTS_PER_GROUP = 8
N_EXPERTS = N_GROUPS * EXPERTS_PER_GROUP
EXPERT_TOPK = 2
EXPERT_HIDDEN = 512
EXPERT_ROW_BLOCK = 128
RMS_EPS = 1e-6
ATTN_OFF = 0
ATTN_COLS = 3 * ATTN_WIDTH
GATE_OFF = ATTN_OFF + ATTN_COLS
GATE_COLS = N_BRANCH * D_MODEL
POOL_OFF = GATE_OFF + GATE_COLS
RWKV_OFF = POOL_OFF + POOL_WIDTH
RWKV_COLS = 3 * RWKV_WIDTH + DECAY_LORA + ICLR_LORA + GATE_LORA
IN_COLS = RWKV_OFF + RWKV_COLS

kernel_name = 'hybrid_moba_rwkv7_pool_hiermoe'


def rmsnorm(x, g):
    xf = x.astype(jnp.float32)
    y = xf * lax.rsqrt(jnp.mean(xf * xf, axis=-1, keepdims=True) + RMS_EPS)
    return (y * g.astype(jnp.float32)).astype(x.dtype)


def alibi_slopes(n_heads):
    return jnp.asarray([2.0 ** (-8.0 * (i + 1) / n_heads) for i in range(n_heads)], dtype=jnp.float32)


def token_shift(z, mu):
    prev = jnp.pad(z[:, :-1], ((0, 0), (1, 0), (0, 0)))
    return z + (prev - z) * mu


def moba_attention(q, k, v, slopes):
    B, H, S, Dh = q.shape
    L = MOBA_BLOCK
    nb = -(-S // L)
    n_cand = max(nb, MOBA_TOPK)
    pad = nb * L - S
    kp = jnp.pad(k, ((0, 0), (0, 0), (0, pad), (0, 0)))
    vp = jnp.pad(v, ((0, 0), (0, 0), (0, pad), (0, 0)))
    kb = kp.reshape(B, H, nb, L, Dh)
    vb = vp.reshape(B, H, nb, L, Dh)
    kmean = jnp.mean(kb.astype(jnp.float32), axis=3)
    kmean = jnp.pad(kmean, ((0, 0), (0, 0), (0, n_cand - nb), (0, 0)))
    scale = Dh ** -0.5
    bi = jnp.arange(B)[:, None, None, None]
    hi = jnp.arange(H)[None, :, None, None]
    cand = jnp.arange(n_cand)
    offs = jnp.arange(L)
    slope5 = slopes[None, :, None, None, None]
    slope4 = slopes[None, :, None, None]

    def one_chunk(c):
        q0 = c * QUERY_CHUNK
        own = q0 // L
        qc = lax.dynamic_slice_in_dim(q, q0, QUERY_CHUNK, axis=2).astype(jnp.float32)
        tq = q0 + jnp.arange(QUERY_CHUNK)
        gate = jnp.einsum('bhqd,bhnd->bhqn', qc, kmean)
        gate = jnp.where(cand < own, gate, -jnp.inf)
        _, sel = lax.top_k(gate, MOBA_TOPK)
        valid = sel < own
        sel = jnp.minimum(sel, nb - 1)
        k_sel = kb[bi, hi, sel].astype(jnp.float32)
        dist = (tq[:, None, None] - (sel[..., None] * L + offs)).astype(jnp.float32)
        s_sel = jnp.einsum('bhqd,bhqjld->bhqjl', qc, k_sel) * scale - slope5 * dist
        s_sel = jnp.where(valid[..., None], s_sel, -jnp.inf)
        s_sel = s_sel.reshape(B, H, QUERY_CHUNK, MOBA_TOPK * L)
        k_own = lax.dynamic_slice_in_dim(kp, own * L, L, axis=2).astype(jnp.float32)
        v_own = lax.dynamic_slice_in_dim(vp, own * L, L, axis=2).astype(jnp.float32)
        dist_own = tq[:, None] - (own * L + offs)[None, :]
        s_own = jnp.einsum('bhqd,bhld->bhql', qc, k_own) * scale - slope4 * dist_own.astype(jnp.float32)
        s_own = jnp.where(dist_own >= 0, s_own, -jnp.inf)
        p = jax.nn.softmax(jnp.concatenate([s_sel, s_own], axis=-1), axis=-1)
        p_sel = p[..., :MOBA_TOPK * L].reshape(B, H, QUERY_CHUNK, MOBA_TOPK, L)
        v_sel = vb[bi, hi, sel].astype(jnp.float32)
        o = (jnp.einsum('bhqjl,bhqjld->bhqd', p_sel, v_sel)
             + jnp.einsum('bhql,bhld->bhqd', p[..., MOBA_TOPK * L:], v_own))
        return o.astype(q.dtype)

    out = lax.map(one_chunk, jnp.arange(S // QUERY_CHUNK))
    return jnp.moveaxis(out, 0, 2).reshape(B, H, S, Dh)


def wkv7_scan(r, w, k, v, a, b):
    B, S, H, N = r.shape

    def step(state, inp):
        r_t, w_t, k_t, v_t, a_t, b_t = inp
        sa = jnp.einsum('bhvk,bhk->bhv', state, a_t)
        state = (state * w_t[:, :, None, :] + sa[..., None] * b_t[:, :, None, :]
                 + v_t[..., None] * k_t[:, :, None, :])
        return state, jnp.einsum('bhvk,bhk->bhv', state, r_t)

    xs = (jnp.swapaxes(r, 0, 1), jnp.swapaxes(w, 0, 1), jnp.swapaxes(k, 0, 1),
          jnp.swapaxes(v, 0, 1), jnp.swapaxes(a, 0, 1), jnp.swapaxes(b, 0, 1))
    _, y = lax.scan(step, jnp.zeros((B, H, N, N), jnp.float32), xs)
    return jnp.swapaxes(y, 0, 1)


def rwkv7_time_mix(cols, v_first, w0, w_decay_up, a0, w_iclr_up, w_gate_up, k_k, k_a, r_k,
                   lnx_g, lnx_b, vmix):
    B, S, _ = cols.shape
    W = RWKV_WIDTH
    c = cols.astype(jnp.float32)
    o1 = 3 * W
    o2 = o1 + DECAY_LORA
    o3 = o2 + ICLR_LORA
    o4 = o3 + GATE_LORA
    r, k, v = c[..., :W], c[..., W:2 * W], c[..., 2 * W:o1]
    w_lo, a_lo, g_lo = c[..., o1:o2], c[..., o2:o3], c[..., o3:o4]

    def heads(t):
        return t.reshape(B, S, RWKV_HEADS, RWKV_HEAD_DIM)

    w_log = -jax.nn.softplus(-(w0 + jnp.tanh(w_lo) @ w_decay_up)) - 0.5
    decay = jnp.exp(-jnp.exp(w_log))
    if vmix is None:
        v_first = v
    else:
        vm_lo, v0, w_vmix_up = vmix
        v = v + (v_first - v) * jax.nn.sigmoid(v0 + vm_lo.astype(jnp.float32) @ w_vmix_up)
    a = jax.nn.sigmoid(a0 + a_lo @ w_iclr_up)
    g = jax.nn.sigmoid(g_lo) @ w_gate_up
    kk = heads(k * k_k)
    kk = kk / jnp.maximum(jnp.sqrt(jnp.sum(kk * kk, axis=-1, keepdims=True)), 1e-12)
    k = k * (1.0 + (a - 1.0) * k_a)
    r_h, k_h, v_h, a_h = heads(r), heads(k), heads(v), heads(a)
    y = wkv7_scan(r_h, heads(decay), k_h, v_h, -kk, kk * a_h)
    mu = jnp.mean(y, axis=-1, keepdims=True)
    var = jnp.mean(jnp.square(y - mu), axis=-1, keepdims=True)
    y = ((y - mu) * lax.rsqrt(var + LNX_EPS)).reshape(B, S, W) * lnx_g + lnx_b
    bonus = jnp.sum(r_h * k_h * r_k, axis=-1, keepdims=True) * v_h
    y = (y + bonus.reshape(B, S, W)) * g
    return y.astype(cols.dtype), v_first


def multiscale_pool(z, pool_w, pool_scale):
    B, S, _ = z.shape
    zf = z.astype(jnp.float32).reshape(B, S, POOL_GROUPS, POOL_GROUP_DIM)
    cum = jnp.cumsum(zf, axis=1)
    t1 = jnp.arange(1, S + 1, dtype=jnp.float32)[None, :, None]
    outs = []
    for gi, win in enumerate(POOL_WINDOWS):
        cg = cum[:, :, gi]
        lagged = jnp.pad(cg, ((0, 0), (win, 0), (0, 0)))[:, :S]
        outs.append((cg - lagged) / jnp.minimum(t1, float(win)) - zf[:, :, gi])
    p = jnp.stack(outs, axis=2)
    y = jnp.einsum('bsgc,gcd->bsgd', p, pool_w.astype(jnp.float32)).reshape(B, S, POOL_WIDTH)
    return (y * pool_scale).astype(z.dtype)


def hier_moe(h, rg_w, rg_b, re_w, re_b, w1, w3, w2):
    B, S, D = h.shape
    T = B * S
    hf = h.reshape(T, D)
    glog = (hf @ rg_w).astype(jnp.float32) + rg_b
    gprob = jax.nn.softmax(glog, axis=-1)
    grp = jnp.argmax(glog, axis=-1)
    elog = ((hf @ re_w).astype(jnp.float32) + re_b).reshape(T, N_GROUPS, EXPERTS_PER_GROUP)
    elog_g = jnp.take_along_axis(elog, grp[:, None, None], axis=1)[:, 0]
    top_v, top_i = lax.top_k(elog_g, EXPERT_TOPK)
    wts = jax.nn.softmax(top_v, axis=-1) * jnp.take_along_axis(gprob, grp[:, None], axis=1)
    eid = (grp[:, None] * EXPERTS_PER_GROUP + top_i).reshape(-1)
    tok = jnp.repeat(jnp.arange(T), EXPERT_TOPK)
    wts = wts.reshape(-1)
    A = T * EXPERT_TOPK
    blk = EXPERT_ROW_BLOCK
    order = jnp.argsort(eid)
    e_s, tok_s, w_s = eid[order], tok[order], wts[order]
    counts = jnp.bincount(eid, length=N_EXPERTS)
    starts = jnp.cumsum(counts) - counts
    padded = (counts + blk - 1) // blk * blk
    pends = jnp.cumsum(padded)
    pstarts = pends - padded
    dest = pstarts[e_s] + (jnp.arange(A) - starts[e_s])
    R = -(-(A + N_EXPERTS * (blk - 1)) // blk) * blk
    nblk = R // blk
    buf = jnp.zeros((R, D), h.dtype).at[dest].set(hf[tok_s])
    blk_e = jnp.minimum(jnp.searchsorted(pends, jnp.arange(nblk) * blk, side='right'), N_EXPERTS - 1)

    def expert_block(args):
        xb, e = args
        return (jax.nn.silu(xb @ w1[e]) * (xb @ w3[e])) @ w2[e]

    yb = lax.map(expert_block, (buf.reshape(nblk, blk, D), blk_e)).reshape(R, D)
    y = yb[dest] * w_s[:, None].astype(yb.dtype)
    out = jax.ops.segment_sum(y, tok_s, num_segments=T)
    return out.reshape(B, S, D).astype(h.dtype)


def setup_inputs(seed: int = 0) -> dict:
    key = jax.random.key(seed)
    ks = jax.random.split(key, 32)
    f32 = jnp.float32
    L = DEPTH
    Lr = DEPTH - 1

    def nrm(k, shape, scale):
        return jax.random.normal(k, shape, f32) * scale

    def near_one(k, shape):
        return 1.0 + 0.1 * jax.random.normal(k, shape, f32)

    return {
        'x': nrm(ks[0], (BATCH, SEQ, D_MODEL), 1.0),
        'norm1_g': near_one(ks[1], (L, D_MODEL)),
        'w_in': nrm(ks[2], (L, D_MODEL, IN_COLS), D_MODEL ** -0.5),
        'w_vmix': nrm(ks[3], (Lr, D_MODEL, VMIX_LORA), D_MODEL ** -0.5),
        'rwkv_mu': jax.random.uniform(ks[4], (L, RWKV_COLS), f32),
        'vmix_mu': jax.random.uniform(ks[5], (Lr, VMIX_LORA), f32),
        'w0': jax.random.uniform(ks[6], (L, RWKV_WIDTH), f32, minval=-3.0, maxval=1.0),
        'w_decay_up': nrm(ks[7], (L, DECAY_LORA, RWKV_WIDTH), 0.1),
        'a0': nrm(ks[8], (L, RWKV_WIDTH), 0.5),
        'w_iclr_up': nrm(ks[9], (L, ICLR_LORA, RWKV_WIDTH), ICLR_LORA ** -0.5),
        'w_gate_up': nrm(ks[10], (L, GATE_LORA, RWKV_WIDTH), GATE_LORA ** -0.5),
        'k_k': 0.85 + 0.1 * jax.random.normal(ks[11], (L, RWKV_WIDTH), f32),
        'k_a': near_one(ks[12], (L, RWKV_WIDTH)),
        'r_k': nrm(ks[13], (L, RWKV_HEADS, RWKV_HEAD_DIM), 0.1),
        'lnx_g': near_one(ks[14], (L, RWKV_WIDTH)),
        'lnx_b': nrm(ks[15], (L, RWKV_WIDTH), 0.01),
        'v0': nrm(ks[16], (Lr, RWKV_WIDTH), 0.5),
        'w_vmix_up': nrm(ks[17], (Lr, VMIX_LORA, RWKV_WIDTH), VMIX_LORA ** -0.5),
        'pool_w': nrm(ks[18], (L, POOL_GROUPS, POOL_GROUP_DIM, POOL_GROUP_DIM), POOL_GROUP_DIM ** -0.5),
        'pool_scale': near_one(ks[19], (L, POOL_WIDTH)),
        'w_branch': nrm(ks[20], (L, N_BRANCH, BRANCH_WIDTH, D_MODEL), BRANCH_WIDTH ** -0.5),
        'w_out': nrm(ks[21], (L, D_MODEL, D_MODEL), D_MODEL ** -0.5),
        'norm2_g': near_one(ks[22], (L, D_MODEL)),
        'router_grp_w': nrm(ks[23], (L, D_MODEL, N_GROUPS), D_MODEL ** -0.5),
        'router_grp_b': nrm(ks[24], (L, N_GROUPS), 0.01),
        'router_exp_w': nrm(ks[25], (L, D_MODEL, N_EXPERTS), D_MODEL ** -0.5),
        'router_exp_b': nrm(ks[26], (L, N_EXPERTS), 0.01),
        'exp_w1': nrm(ks[27], (L, N_EXPERTS, D_MODEL, EXPERT_HIDDEN), D_MODEL ** -0.5),
        'exp_w3': nrm(ks[28], (L, N_EXPERTS, D_MODEL, EXPERT_HIDDEN), D_MODEL ** -0.5),
        'exp_w2': nrm(ks[29], (L, N_EXPERTS, EXPERT_HIDDEN, D_MODEL), EXPERT_HIDDEN ** -0.5),
        'final_norm_g': near_one(ks[30], (D_MODEL,)),
    }


def reference(x, norm1_g, w_in, w_vmix, rwkv_mu, vmix_mu, w0, w_decay_up, a0, w_iclr_up,
              w_gate_up, k_k, k_a, r_k, lnx_g, lnx_b, v0, w_vmix_up, pool_w, pool_scale,
              w_branch, w_out, norm2_g, router_grp_w, router_grp_b, router_exp_w, router_exp_b,
              exp_w1, exp_w3, exp_w2, final_norm_g):
    B, S, D = x.shape
    slopes = alibi_slopes(ATTN_HEADS)
    v_first = None

    def attn_heads(t):
        return t.reshape(B, S, ATTN_HEADS, ATTN_HEAD_DIM).transpose(0, 2, 1, 3)

    for l in range(DEPTH):
        h = rmsnorm(x, norm1_g[l])
        if l == 0:
            w_cols, mu = w_in[l], rwkv_mu[l]
        else:
            w_cols = jnp.concatenate([w_in[l], w_vmix[l - 1]], axis=1)
            mu = jnp.concatenate([rwkv_mu[l], vmix_mu[l - 1]], axis=0)
        z = h @ w_cols
        q = attn_heads(z[..., ATTN_OFF:ATTN_OFF + ATTN_WIDTH])
        k = attn_heads(z[..., ATTN_OFF + ATTN_WIDTH:ATTN_OFF + 2 * ATTN_WIDTH])
        v = attn_heads(z[..., ATTN_OFF + 2 * ATTN_WIDTH:ATTN_OFF + 3 * ATTN_WIDTH])
        y_attn = moba_attention(q, k, v, slopes).transpose(0, 2, 1, 3).reshape(B, S, ATTN_WIDTH)
        rw = token_shift(z[..., RWKV_OFF:], mu)
        vmix = None if l == 0 else (rw[..., RWKV_COLS:], v0[l - 1], w_vmix_up[l - 1])
        y_rwkv, v_first = rwkv7_time_mix(rw[..., :RWKV_COLS], v_first, w0[l], w_decay_up[l], a0[l],
                                         w_iclr_up[l], w_gate_up[l], k_k[l], k_a[l], r_k[l],
                                         lnx_g[l], lnx_b[l], vmix)
        y_pool = multiscale_pool(z[..., POOL_OFF:POOL_OFF + POOL_WIDTH], pool_w[l], pool_scale[l])
        gates = jax.nn.sigmoid(z[..., GATE_OFF:GATE_OFF + GATE_COLS].astype(jnp.float32))
        gates = gates.reshape(B, S, N_BRANCH, D)
        ys = jnp.stack([y_attn.astype(x.dtype), y_rwkv.astype(x.dtype), y_pool.astype(x.dtype)], axis=2)
        proj = jnp.einsum('bsiw,iwd->bsid', ys, w_branch[l])
        merged = jnp.sum(gates * proj, axis=2).astype(x.dtype)
        x = x + merged @ w_out[l]
        h2 = rmsnorm(x, norm2_g[l])
        x = x + hier_moe(h2, router_grp_w[l], router_grp_b[l], router_exp_w[l], router_exp_b[l],
                         exp_w1[l], exp_w3[l], exp_w2[l])
    return rmsnorm(x, final_norm_g)
```

```python
import functools

import jax
import jax.numpy as jnp
from jax import lax
from jax.experimental import pallas as pl
from jax.experimental.pallas import tpu as pltpu

F32 = jnp.float32
BF16 = jnp.bfloat16
HIGHEST = lax.Precision.HIGHEST

D_MODEL = 1024
DEPTH = 2
BRANCH_WIDTH = 512
N_BRANCH = 3
ATTN_HEADS = 8
HEAD_DIM = 64
MOBA_BLOCK = 256
MOBA_TOPK = 3
DECAY_LORA = 64
ICLR_LORA = 64
GATE_LORA = 128
VMIX_LORA = 32
LNX_EPS = 64e-5
POOL_WINDOWS = (2, 4, 8, 16)
POOL_GROUP_DIM = 128
N_GROUPS = 4
EXPERTS_PER_GROUP = 8
N_EXPERTS = 32
EXPERT_TOPK = 2
EXPERT_HIDDEN = 512
EXPERT_ROW_BLOCK = 128
RMS_EPS = 1e-6
ATTN_COLS = 3 * BRANCH_WIDTH
GATE_OFF = ATTN_COLS
GATE_COLS = N_BRANCH * D_MODEL
POOL_OFF = GATE_OFF + GATE_COLS
RWKV_OFF = POOL_OFF + BRANCH_WIDTH

LANES = 128
TM = 256
CHUNK = 64
N_CHUNK = TM // CHUNK
N_PAIR = ATTN_HEADS // 2
RW_COLS = 2048
POOL_HALO = 16
VMEM_LIMIT = 56 * 1024 * 1024


def _cparams(sem):
    return pltpu.CompilerParams(dimension_semantics=sem, vmem_limit_bytes=VMEM_LIMIT)


def _const_spec(shape):
    nd = len(shape)
    return pl.BlockSpec(shape, lambda *a: (0,) * nd, pipeline_mode=pl.Buffered(1))


def _sigmoid(x):
    return 1.0 / (1.0 + jnp.exp(-x))


def _rms(x, g):
    return x * lax.rsqrt(jnp.mean(x * x, axis=-1, keepdims=True) + RMS_EPS) * g


def _split3(x):
    h1 = x.astype(BF16)
    r1 = x - h1.astype(F32)
    h2 = r1.astype(BF16)
    h3 = (r1 - h2.astype(F32)).astype(BF16)
    return h1, h2, h3


def _dot(a, b):
    return jnp.dot(a, b, preferred_element_type=F32)


def _dot_nt(a, b):
    return lax.dot_general(a, b, (((1,), (1,)), ((), ())), preferred_element_type=F32)


def _dot_tn(a, b):
    return lax.dot_general(a, b, (((0,), (0,)), ((), ())), preferred_element_type=F32)


def _inproj_body(x_ref, g_ref, wa_ref, wg_ref, wp_ref, wr_ref,
                 q_ref, k_ref, v_ref, gate_ref, zp_ref, zr_ref, km_ref):
    hb = _rms(x_ref[...], g_ref[...]).astype(BF16)
    za = _dot(hb, wa_ref[...])
    q_ref[...] = (za[:, :BRANCH_WIDTH] * (HEAD_DIM ** -0.5)).astype(BF16)
    k = za[:, BRANCH_WIDTH:2 * BRANCH_WIDTH]
    k_ref[...] = k.astype(BF16)
    v_ref[...] = za[:, 2 * BRANCH_WIDTH:].astype(BF16)
    km_ref[0] = jnp.mean(k, axis=0, keepdims=True)
    gate_ref[...] = _sigmoid(_dot(hb, wg_ref[...]))
    zp_ref[...] = _dot(hb, wp_ref[...])
    zr_ref[...] = _dot(hb, wr_ref[...])


def _inproj(xf, g, wa, wg, wp, wr):
    T = xf.shape[0]
    nt = T // TM
    row = lambda w: pl.BlockSpec((TM, w), lambda i: (i, 0))
    return pl.pallas_call(
        _inproj_body,
        grid=(nt,),
        in_specs=[row(D_MODEL), _const_spec((1, D_MODEL)), _const_spec(wa.shape), _const_spec(wg.shape),
                  _const_spec(wp.shape), _const_spec(wr.shape)],
        out_specs=[row(BRANCH_WIDTH), row(BRANCH_WIDTH), row(BRANCH_WIDTH), row(GATE_COLS),
                   row(BRANCH_WIDTH), row(RW_COLS), pl.BlockSpec((1, 1, BRANCH_WIDTH), lambda i: (i, 0, 0))],
        out_shape=[jax.ShapeDtypeStruct((T, BRANCH_WIDTH), BF16)] * 3
        + [jax.ShapeDtypeStruct((T, GATE_COLS), F32), jax.ShapeDtypeStruct((T, BRANCH_WIDTH), F32),
           jax.ShapeDtypeStruct((T, RW_COLS), F32), jax.ShapeDtypeStruct((nt, 1, BRANCH_WIDTH), F32)],
        compiler_params=_cparams(("parallel",)),
        name="inproj",
    )(xf, g, wa, wg, wp, wr)


def _select_body(q_ref, km_ref, o_ref):
    i = pl.program_id(2)
    q = q_ref[...].astype(F32)
    km = km_ref[0]
    nb = km.shape[0]
    lane = lax.broadcasted_iota(jnp.int32, (1, LANES), 1)
    blk = lax.broadcasted_iota(jnp.int32, (nb, TM), 0)
    rows = []
    for half in range(2):
        in_head = (lane < HEAD_DIM) if half == 0 else (lane >= HEAD_DIM)
        g = lax.dot_general(jnp.where(in_head, km, 0.0), q, (((1,), (1,)), ((), ())),
                            precision=HIGHEST, preferred_element_type=F32)
        g = jnp.where(blk < i, g, -jnp.inf)
        for _ in range(MOBA_TOPK):
            mx = jnp.max(g, axis=0, keepdims=True)
            idx = jnp.min(jnp.where(g == mx, blk, nb), axis=0, keepdims=True)
            rows.append(jnp.where(mx > -jnp.inf, idx, -1))
            g = jnp.where(blk == idx, -jnp.inf, g)
    rows += [jnp.full((1, TM), -1, jnp.int32)] * 2
    ids = jnp.concatenate(rows, axis=0).astype(F32)
    eye = (lax.broadcasted_iota(jnp.int32, (TM, TM), 0)
           == lax.broadcasted_iota(jnp.int32, (TM, TM), 1)).astype(BF16)
    o_ref[...] = _dot_nt(eye, ids.astype(BF16)).astype(jnp.int32)


def _moba_select(q, kmean, B, S):
    ns = S // TM
    nb = kmean.shape[1]
    return pl.pallas_call(
        _select_body,
        grid=(B, N_PAIR, ns),
        in_specs=[pl.BlockSpec((TM, LANES), lambda b, p, i: (b * ns + i, p)),
                  pl.BlockSpec((1, nb, LANES), lambda b, p, i: (b, 0, p))],
        out_specs=pl.BlockSpec((TM, 8), lambda b, p, i: ((b * N_PAIR + p) * ns + i, 0)),
        out_shape=jax.ShapeDtypeStruct((B * N_PAIR * S, 8), jnp.int32),
        compiler_params=_cparams(("parallel", "parallel", "parallel")),
        name="moba_select",
    )(q, kmean)


def _attn_body(ids_ref, q_ref, k_ref, v_ref, slope_ref, ccol_ref, o_ref, qq_ref, m_ref, acc_ref):
    i = pl.program_id(2)
    lane = lax.broadcasted_iota(jnp.int32, (1, LANES), 1)
    first = lane < HEAD_DIM
    q = q_ref[...]
    zero = jnp.zeros_like(q)
    qq_ref[0:TM, 0:LANES] = jnp.where(first, q, zero)
    qq_ref[TM:, 0:LANES] = jnp.where(first, zero, q)
    qq_ref[:, LANES:] = slope_ref[0]
    slope = slope_ref[0][:, 0:1].astype(F32)
    ids = ids_ref[...]
    sel3 = jnp.concatenate([ids[:, 0:MOBA_TOPK], ids[:, MOBA_TOPK:2 * MOBA_TOPK]], axis=0)
    ccol = ccol_ref[...]
    one = jnp.ones((TM, LANES), BF16)

    def scores(j):
        kj = k_ref[pl.ds(pl.multiple_of(j * TM, TM), TM), :]
        return _dot_nt(qq_ref[...], jnp.concatenate([kj, ccol], axis=1))

    def pv(p, j):
        vj = v_ref[pl.ds(pl.multiple_of(j * TM, TM), TM), :]
        va = jnp.where(first, vj, one)
        vb = jnp.where(first, one, vj)
        pb = p.astype(BF16)
        return jnp.concatenate([_dot(pb[:TM], va), _dot(pb[TM:], vb)], axis=0)

    s = scores(i)
    r = lax.broadcasted_iota(jnp.int32, (2 * TM, TM), 0)
    r = jnp.where(r >= TM, r - TM, r)
    c = lax.broadcasted_iota(jnp.int32, (2 * TM, TM), 1)
    s = jnp.where(r >= c, s, -jnp.inf)
    m0 = jnp.max(s, axis=-1, keepdims=True)
    m_ref[...] = m0
    acc_ref[...] = pv(jnp.exp(s - m0), i)

    def body(j, carry):
        s = scores(j)
        cj = slope * ((j - i) * TM).astype(F32)
        sel = (sel3[:, 0:1] == j) | (sel3[:, 1:2] == j) | (sel3[:, 2:3] == j)
        m_old = m_ref[...]
        m_new = jnp.where(sel, jnp.maximum(m_old, jnp.max(s, axis=-1, keepdims=True) + cj), m_old)
        alpha = jnp.exp(m_old - m_new)
        shift = jnp.where(sel, m_new - cj, jnp.inf)
        acc_ref[...] = acc_ref[...] * alpha + pv(jnp.exp(s - shift), j)
        m_ref[...] = m_new
        return carry

    lax.fori_loop(0, i, body, 0)
    acc = acc_ref[...]
    out = acc / pltpu.roll(acc, HEAD_DIM, axis=1)
    o_ref[...] = jnp.where(first, out[:TM], out[TM:]).astype(BF16)


def _moba_attn(ids, q, k, v, slope_cols, ccol, B, S):
    ns = S // TM
    return pl.pallas_call(
        _attn_body,
        grid=(B, N_PAIR, ns),
        in_specs=[pl.BlockSpec((TM, 8), lambda b, p, i: ((b * N_PAIR + p) * ns + i, 0)),
                  pl.BlockSpec((TM, LANES), lambda b, p, i: (b * ns + i, p)),
                  pl.BlockSpec((S, LANES), lambda b, p, i: (b, p)),
                  pl.BlockSpec((S, LANES), lambda b, p, i: (b, p)),
                  pl.BlockSpec((1, 2 * TM, LANES), lambda b, p, i: (p, 0, 0)),
                  pl.BlockSpec((TM, LANES), lambda b, p, i: (0, 0))],
        out_specs=pl.BlockSpec((TM, LANES), lambda b, p, i: (b * ns + i, p)),
        out_shape=jax.ShapeDtypeStruct((B * S, BRANCH_WIDTH), BF16),
        scratch_shapes=[pltpu.VMEM((2 * TM, 2 * LANES), BF16), pltpu.VMEM((2 * TM, 1), F32),
                        pltpu.VMEM((2 * TM, LANES), F32)],
        compiler_params=_cparams(("parallel", "parallel", "arbitrary")),
        name="moba_attn",
    )(ids, q, k, v, slope_cols, ccol)


def _pool_body(z_ref, halo_ref, w_ref, sc_ref, o_ref, ext_ref):
    i = pl.program_id(1)
    ext_ref[0:POOL_HALO, :] = jnp.where(i == 0, 0.0, halo_ref[...])
    ext_ref[POOL_HALO:, :] = z_ref[...]
    t1 = (i * TM + 1 + lax.broadcasted_iota(jnp.int32, (TM, 1), 0)).astype(F32)
    for gi, win in enumerate(POOL_WINDOWS):
        cols = slice(gi * POOL_GROUP_DIM, (gi + 1) * POOL_GROUP_DIM)
        acc = ext_ref[POOL_HALO:, cols]
        for d in range(1, win):
            acc = acc + ext_ref[POOL_HALO - d:POOL_HALO - d + TM, cols]
        p = acc / jnp.minimum(t1, float(win)) - ext_ref[POOL_HALO:, cols]
        y = _dot(p.astype(BF16), w_ref[gi])
        o_ref[:, cols] = (y * sc_ref[:, cols]).astype(BF16)


def _pool(zp, w, sc, B, S):
    ns = S // TM
    hb = TM // POOL_HALO
    return pl.pallas_call(
        _pool_body,
        grid=(B, ns),
        in_specs=[pl.BlockSpec((TM, BRANCH_WIDTH), lambda b, i: (b * ns + i, 0)),
                  pl.BlockSpec((POOL_HALO, BRANCH_WIDTH), lambda b, i: (jnp.maximum((b * ns + i) * hb - 1, 0), 0)),
                  _const_spec(w.shape), _const_spec(sc.shape)],
        out_specs=pl.BlockSpec((TM, BRANCH_WIDTH), lambda b, i: (b * ns + i, 0)),
        out_shape=jax.ShapeDtypeStruct((B * S, BRANCH_WIDTH), BF16),
        scratch_shapes=[pltpu.VMEM((POOL_HALO + TM, BRANCH_WIDTH), F32)],
        compiler_params=_cparams(("parallel", "parallel")),
        name="pool",
    )(zp, zp, w, sc)


def _seg_sum(x, seg):
    hi = x.astype(BF16)
    lo = (x - hi.astype(F32)).astype(BF16)
    outs = []
    for p in range(N_PAIR):
        cols = slice(p * LANES, (p + 1) * LANES)
        outs.append(_dot(hi[:, cols], seg) + _dot(lo[:, cols], seg))
    return jnp.concatenate(outs, axis=1)


def _rwkv_prep_body(has_vmix, z_ref, halo_ref, vf_ref, mu_ref, par_ref, wd_ref, wa_ref, wg_ref, wvm_ref,
                    rp_ref, y0_ref, g_ref, h_ref, bonus_ref, gate_ref, v_ref):
    i = pl.program_id(1)
    W = BRANCH_WIDTH
    z = z_ref[...]
    prev0 = jnp.where(i == 0, 0.0, halo_ref[7:8, :])
    row = lax.broadcasted_iota(jnp.int32, (TM, 1), 0)
    prev = jnp.where(row == 0, prev0, pltpu.roll(z, 1, axis=0))
    rw = z + (prev - z) * mu_ref[...]
    r, k, v = rw[:, 0:W], rw[:, W:2 * W], rw[:, 2 * W:3 * W]
    w_lo = rw[:, 3 * W:3 * W + LANES]
    a_lo = rw[:, 3 * W + LANES:3 * W + 2 * LANES]
    g_lo = rw[:, 3 * W + 2 * LANES:3 * W + 3 * LANES]
    vm_lo = rw[:, 3 * W + 3 * LANES:3 * W + 4 * LANES]
    w0, a0, k_k, k_a, r_k, v0 = (par_ref[n:n + 1, :] for n in range(6))
    hdot = functools.partial(jnp.dot, precision=HIGHEST, preferred_element_type=F32)

    u = w0 + hdot(jnp.tanh(w_lo), wd_ref[...])
    nu = -u
    softplus = jnp.maximum(nu, 0.0) + jnp.log(1.0 + jnp.exp(-jnp.abs(nu)))
    logw = -jnp.exp(-softplus - 0.5)
    if has_vmix:
        v = v + (vf_ref[...] - v) * _sigmoid(v0 + hdot(vm_lo, wvm_ref[...]))
    v_ref[...] = v
    a = _sigmoid(a0 + hdot(a_lo, wa_ref[...]))
    gate_ref[...] = hdot(_sigmoid(g_lo), wg_ref[...])

    lane = lax.broadcasted_iota(jnp.int32, (1, LANES), 1)
    first = lane < HEAD_DIM
    seg = ((lax.broadcasted_iota(jnp.int32, (LANES, LANES), 0) < HEAD_DIM)
           == (lax.broadcasted_iota(jnp.int32, (LANES, LANES), 1) < HEAD_DIM)).astype(BF16)
    kk = k * k_k
    kk = kk / jnp.maximum(jnp.sqrt(_seg_sum(kk * kk, seg)), 1e-12)
    k2 = k * (1.0 + (a - 1.0) * k_a)
    bonus_ref[...] = _seg_sum(r * k2 * r_k, seg) * v

    ri = lax.broadcasted_iota(jnp.int32, (TM, TM), 0)
    ci = lax.broadcasted_iota(jnp.int32, (TM, TM), 1)
    same = (ri // CHUNK) == (ci // CHUNK)
    incl = same & (ri >= ci)
    strict = same & (ri > ci)
    lmat = incl.astype(BF16)
    h1, h2, h3 = _split3(logw)
    cw = _dot(lmat, h1) + _dot(lmat, h2) + _dot(lmat, h3)
    e_neg = jnp.exp(-cw)
    at = (-kk) * jnp.exp(cw - logw)
    bt = (kk * a) * e_neg
    kt = k2 * e_neg
    rt = r * jnp.exp(cw)
    eye = (ri == ci).astype(F32)
    bd = ((lax.broadcasted_iota(jnp.int32, (LANES, LANES), 0) < HEAD_DIM)
          == (lax.broadcasted_iota(jnp.int32, (LANES, LANES), 1) < HEAD_DIM))
    eye_l = (lax.broadcasted_iota(jnp.int32, (LANES, LANES), 0)
             == lax.broadcasted_iota(jnp.int32, (LANES, LANES), 1)).astype(F32)

    for p in range(N_PAIR):
        cols = slice(p * LANES, (p + 1) * LANES)
        ap, bp, kp, rp, vp = at[:, cols], bt[:, cols], kt[:, cols], rt[:, cols], v[:, cols]
        bk = jnp.concatenate([bp, kp], axis=0).astype(BF16)
        p_pair = jnp.zeros((TM, LANES), F32)
        q_pair = jnp.zeros((TM, LANES), F32)
        rp_pair = jnp.zeros((TM, LANES), F32)
        y0_pair = jnp.zeros((TM, LANES), F32)
        for half in range(2):
            mh = first if half == 0 else jnp.logical_not(first)
            am = jnp.where(mh, ap, 0.0)
            rm = jnp.where(mh, rp, 0.0)
            vm = jnp.where(mh, vp, 0.0).astype(BF16)
            big = _dot_nt(jnp.concatenate([am, rm], axis=0).astype(BF16), bk)
            n = jnp.where(strict, big[:TM, :TM], 0.0)
            aak = jnp.where(strict, big[:TM, TM:], 0.0)
            mrb = jnp.where(incl, big[TM:, :TM], 0.0)
            mrk = jnp.where(incl, big[TM:, TM:], 0.0)
            tm = eye + n
            npow = n
            steps = CHUNK.bit_length() - 2
            for _ in range(steps):
                nb16 = npow.astype(BF16)
                npow = _dot(nb16, nb16)
                tm = tm + _dot(tm.astype(BF16), npow.astype(BF16))
            av = _dot(aak.astype(BF16), vm)
            x = _dot(tm.astype(BF16), jnp.concatenate([am, av], axis=1).astype(BF16))
            ph, qh = x[:, :LANES], x[:, LANES:]
            mrb16 = mrb.astype(BF16)
            p_pair = p_pair + ph
            q_pair = q_pair + qh
            rp_pair = rp_pair + rm + _dot(mrb16, ph.astype(BF16))
            y0_pair = y0_pair + _dot(mrb16, qh.astype(BF16)) + _dot(mrk.astype(BF16), vm)
        rp_ref[:, cols] = rp_pair
        y0_ref[:, cols] = y0_pair
        for c in range(N_CHUNK):
            rows = slice(c * CHUNK, (c + 1) * CHUNK)
            wc = jnp.exp(cw[(c + 1) * CHUNK - 1:(c + 1) * CHUNK, cols])
            bc = bp[rows].astype(BF16)
            gp = _dot_tn(p_pair[rows].astype(BF16), bc)
            hp = _dot_tn(jnp.concatenate([q_pair[rows], vp[rows]], axis=0).astype(BF16),
                         jnp.concatenate([bp[rows], kp[rows]], axis=0).astype(BF16))
            g_ref[0, c, p] = (eye_l + jnp.where(bd, gp, 0.0)) * wc
            h_ref[0, c, p] = jnp.where(bd, hp, 0.0) * wc


def _rwkv_prep(zr, vfirst, mu, par, wd, wa, wg, wvm, has_vmix, B, S):
    ns = S // TM
    T = B * S
    tile = lambda w: pl.BlockSpec((TM, w), lambda b, i: (b * ns + i, 0))
    gh_spec = pl.BlockSpec((1, N_CHUNK, N_PAIR, LANES, LANES), lambda b, i: (b * ns + i, 0, 0, 0, 0))
    gh_shape = jax.ShapeDtypeStruct((B * ns, N_CHUNK, N_PAIR, LANES, LANES), F32)
    wide = jax.ShapeDtypeStruct((T, BRANCH_WIDTH), F32)
    return pl.pallas_call(
        functools.partial(_rwkv_prep_body, has_vmix),
        grid=(B, ns),
        in_specs=[tile(RW_COLS),
                  pl.BlockSpec((8, RW_COLS), lambda b, i: (jnp.maximum((b * ns + i) * (TM // 8) - 1, 0), 0)),
                  tile(BRANCH_WIDTH), _const_spec(mu.shape), _const_spec(par.shape), _const_spec(wd.shape),
                  _const_spec(wa.shape), _const_spec(wg.shape), _const_spec(wvm.shape)],
        out_specs=[tile(BRANCH_WIDTH), tile(BRANCH_WIDTH), gh_spec, gh_spec,
                   tile(BRANCH_WIDTH), tile(BRANCH_WIDTH), tile(BRANCH_WIDTH)],
        out_shape=[wide, wide, gh_shape, gh_shape, wide, wide, wide],
        compiler_params=_cparams(("parallel", "parallel")),
        name="rwkv_prep",
    )(zr, zr, vfirst, mu, par, wd, wa, wg, wvm)


def _rwkv_scan_body(rp_ref, y0_ref, g_ref, h_ref, bonus_ref, gate_ref, ln_ref, o_ref, s_ref):
    i = pl.program_id(1)

    @pl.when(i == 0)
    def _():
        s_ref[...] = jnp.zeros_like(s_ref)

    ys = []
    for p in range(N_PAIR):
        cols = slice(p * LANES, (p + 1) * LANES)
        s = s_ref[p]
        yc = []
        for c in range(N_CHUNK):
            rows = slice(c * CHUNK, (c + 1) * CHUNK)
            s16 = s.astype(BF16)
            yc.append(_dot_nt(rp_ref[rows, cols].astype(BF16), s16) + y0_ref[rows, cols])
            s = _dot(s16, g_ref[0, c, p].astype(BF16)) + h_ref[0, c, p]
        s_ref[p] = s
        ys.append(jnp.concatenate(yc, axis=0))
    y = jnp.concatenate(ys, axis=1)
    seg = ((lax.broadcasted_iota(jnp.int32, (LANES, LANES), 0) < HEAD_DIM)
           == (lax.broadcasted_iota(jnp.int32, (LANES, LANES), 1) < HEAD_DIM)).astype(BF16)
    mu = _seg_sum(y, seg) * (1.0 / HEAD_DIM)
    d = y - mu
    var = _seg_sum(d * d, seg) * (1.0 / HEAD_DIM)
    yn = d * lax.rsqrt(var + LNX_EPS) * ln_ref[0:1, :] + ln_ref[1:2, :]
    o_ref[...] = ((yn + bonus_ref[...]) * gate_ref[...]).astype(BF16)


def _rwkv_scan(rp, y0, g, h, bonus, gate, ln, B, S):
    ns = S // TM
    tile = pl.BlockSpec((TM, BRANCH_WIDTH), lambda b, i: (b * ns + i, 0))
    gh_spec = pl.BlockSpec((1, N_CHUNK, N_PAIR, LANES, LANES), lambda b, i: (b * ns + i, 0, 0, 0, 0))
    return pl.pallas_call(
        _rwkv_scan_body,
        grid=(B, ns),
        in_specs=[tile, tile, gh_spec, gh_spec, tile, tile, _const_spec(ln.shape)],
        out_specs=tile,
        out_shape=jax.ShapeDtypeStruct((B * S, BRANCH_WIDTH), BF16),
        scratch_shapes=[pltpu.VMEM((N_PAIR, LANES, LANES), F32)],
        compiler_params=_cparams(("parallel", "arbitrary")),
        name="rwkv_scan",
    )(rp, y0, g, h, bonus, gate, ln)


def _merge_body(x_ref, gate_ref, ya_ref, yr_ref, yp_ref, wb_ref, wo_ref, g2_ref, rw_ref, rb_ref,
                x1_ref, h2_ref, eid_ref, wt_ref, cnt_ref):
    merged = jnp.zeros((TM, D_MODEL), F32)
    for bi, y_ref in enumerate((ya_ref, yr_ref, yp_ref)):
        merged = merged + gate_ref[:, bi * D_MODEL:(bi + 1) * D_MODEL] * _dot(y_ref[...], wb_ref[bi])
    x1 = x_ref[...] + _dot(merged.astype(BF16), wo_ref[...])
    x1_ref[...] = x1
    h2 = _rms(x1, g2_ref[...])
    h2_ref[...] = h2
    hi = h2.astype(BF16)
    lo = (h2 - hi.astype(F32)).astype(BF16)
    logits = _dot(hi, rw_ref[0]) + _dot(lo, rw_ref[0]) + _dot(hi, rw_ref[1]) + rb_ref[...]
    lane = lax.broadcasted_iota(jnp.int32, (TM, LANES), 1)
    ninf = -jnp.inf
    glog = jnp.where(lane < N_GROUPS, logits, ninf)
    gmax = jnp.max(glog, axis=-1, keepdims=True)
    grp = jnp.min(jnp.where(glog == gmax, lane, LANES), axis=-1, keepdims=True)
    gprob = 1.0 / jnp.sum(jnp.exp(glog - gmax), axis=-1, keepdims=True)
    e_lane = lane - N_EXPERTS
    in_grp = (e_lane >= grp * EXPERTS_PER_GROUP) & (e_lane < (grp + 1) * EXPERTS_PER_GROUP)
    elog = jnp.where(in_grp, logits, ninf)
    v1 = jnp.max(elog, axis=-1, keepdims=True)
    i1 = jnp.min(jnp.where(elog == v1, e_lane, LANES), axis=-1, keepdims=True)
    elog = jnp.where(e_lane == i1, ninf, elog)
    v2 = jnp.max(elog, axis=-1, keepdims=True)
    i2 = jnp.min(jnp.where(elog == v2, e_lane, LANES), axis=-1, keepdims=True)
    e2 = jnp.exp(v2 - v1)
    den = 1.0 + e2
    eid_ref[...] = jnp.concatenate([i1, i2], axis=1)
    wt_ref[...] = jnp.concatenate([gprob / den, gprob * e2 / den], axis=1)
    hit = ((lane == i1) | (lane == i2)).astype(F32)
    cnt_ref[0] = jnp.sum(hit, axis=0, keepdims=True).astype(jnp.int32)


def _merge(xf, gates, ya, yr, yp, wb, wo, g2, rw, rb):
    T = xf.shape[0]
    nt = T // TM
    row = lambda w: pl.BlockSpec((TM, w), lambda i: (i, 0))
    return pl.pallas_call(
        _merge_body,
        grid=(nt,),
        in_specs=[row(D_MODEL), row(GATE_COLS), row(BRANCH_WIDTH), row(BRANCH_WIDTH), row(BRANCH_WIDTH),
                  _const_spec(wb.shape), _const_spec(wo.shape), _const_spec(g2.shape), _const_spec(rw.shape),
                  _const_spec(rb.shape)],
        out_specs=[row(D_MODEL), row(D_MODEL), row(EXPERT_TOPK), row(EXPERT_TOPK),
                   pl.BlockSpec((1, 1, LANES), lambda i: (i, 0, 0))],
        out_shape=[jax.ShapeDtypeStruct((T, D_MODEL), F32), jax.ShapeDtypeStruct((T, D_MODEL), F32),
                   jax.ShapeDtypeStruct((T, EXPERT_TOPK), jnp.int32), jax.ShapeDtypeStruct((T, EXPERT_TOPK), F32),
                   jax.ShapeDtypeStruct((nt, 1, LANES), jnp.int32)],
        compiler_params=_cparams(("parallel",)),
        name="merge",
    )(xf, gates, ya, yr, yp, wb, wo, g2, rw, rb)


def _lane_excl_cumsum(x):
    lane = lax.broadcasted_iota(jnp.int32, x.shape, 1)
    inc = x
    sh = 1
    while sh < LANES:
        inc = inc + jnp.where(lane >= sh, pltpu.roll(inc, sh, axis=1), 0)
        sh *= 2
    return inc - x


def _dest_body(eid_ref, cnt_ref, dest_ref, pend_ref, base_ref):
    i = pl.program_id(0)

    @pl.when(i == 0)
    def _():
        total = jnp.sum(cnt_ref[...], axis=0)
        shift = EXPERT_ROW_BLOCK.bit_length() - 1
        padded = jnp.left_shift(jnp.right_shift(total + (EXPERT_ROW_BLOCK - 1), shift), shift)
        pstart = _lane_excl_cumsum(padded)
        base_ref[...] = pstart
        pend_ref[...] = pstart + padded

    lane = lax.broadcasted_iota(jnp.int32, (TM, LANES), 1)
    e0, e1 = eid_ref[:, 0:1], eid_ref[:, 1:2]
    oh0, oh1 = lane == e0, lane == e1
    both = (oh0 | oh1).astype(BF16)
    lower = (lax.broadcasted_iota(jnp.int32, (TM, TM), 0)
             > lax.broadcasted_iota(jnp.int32, (TM, TM), 1)).astype(BF16)
    pos = _dot(lower, both) + base_ref[...].astype(F32)
    d0 = jnp.sum(jnp.where(oh0, pos, 0.0), axis=-1, keepdims=True)
    d1 = jnp.sum(jnp.where(oh1, pos, 0.0), axis=-1, keepdims=True)
    dest_ref[...] = jnp.concatenate([d0, d1], axis=1).astype(jnp.int32)
    base_ref[...] = base_ref[...] + jnp.sum(both.astype(F32), axis=0, keepdims=True).astype(jnp.int32)


def _moe_dest(eid, cnt):
    T = eid.shape[0]
    nt = T // TM
    return pl.pallas_call(
        _dest_body,
        grid=(nt,),
        in_specs=[pl.BlockSpec((TM, EXPERT_TOPK), lambda i: (i, 0)),
                  pl.BlockSpec((nt, 1, LANES), lambda i: (0, 0, 0))],
        out_specs=[pl.BlockSpec((TM, EXPERT_TOPK), lambda i: (i, 0)),
                   pl.BlockSpec((1, LANES), lambda i: (0, 0))],
        out_shape=[jax.ShapeDtypeStruct((T, EXPERT_TOPK), jnp.int32), jax.ShapeDtypeStruct((1, LANES), jnp.int32)],
        scratch_shapes=[pltpu.VMEM((1, LANES), jnp.int32)],
        compiler_params=_cparams(("arbitrary",)),
        name="moe_dest",
    )(eid, cnt)


def _scatter_body(dest_ref, h_ref, buf_in_ref, buf_ref, sem):
    del buf_in_ref
    i = pl.program_id(0)

    def issue(t, carry):
        for s in range(EXPERT_TOPK):
            d = dest_ref[(i * TM + t) * EXPERT_TOPK + s]
            pltpu.make_async_copy(h_ref.at[pl.ds(t, 1), :], buf_ref.at[pl.ds(d, 1), :], sem).start()
        return carry

    lax.fori_loop(0, TM, issue, 0)
    for _ in range(EXPERT_TOPK):
        pltpu.make_async_copy(h_ref, buf_ref.at[pl.ds(0, TM), :], sem).wait()


def _moe_scatter(dest_flat, h2, buf0):
    T = h2.shape[0]
    nt = T // TM
    return pl.pallas_call(
        _scatter_body,
        grid_spec=pltpu.PrefetchScalarGridSpec(
            num_scalar_prefetch=1, grid=(nt,),
            in_specs=[pl.BlockSpec((TM, D_MODEL), lambda i, d: (i, 0)), pl.BlockSpec(memory_space=pl.ANY)],
            out_specs=pl.BlockSpec(memory_space=pl.ANY),
            scratch_shapes=[pltpu.SemaphoreType.DMA(())]),
        out_shape=jax.ShapeDtypeStruct(buf0.shape, F32),
        input_output_aliases={2: 0},
        compiler_params=_cparams(("arbitrary",)),
        name="moe_scatter",
    )(dest_flat, h2, buf0)


def _expert_body(be_ref, x_ref, w1_ref, w3_ref, w2_ref, o_ref):
    del be_ref
    xb = x_ref[...].astype(BF16)
    h1 = _dot(xb, w1_ref[0])
    h3 = _dot(xb, w3_ref[0])
    hid = (h1 * _sigmoid(h1)) * h3
    o_ref[...] = _dot(hid.astype(BF16), w2_ref[0])


def _moe_experts(blk_e, buf, w1, w3, w2):
    R = buf.shape[0]
    nblk = R // EXPERT_ROW_BLOCK
    return pl.pallas_call(
        _expert_body,
        grid_spec=pltpu.PrefetchScalarGridSpec(
            num_scalar_prefetch=1, grid=(nblk,),
            in_specs=[pl.BlockSpec((EXPERT_ROW_BLOCK, D_MODEL), lambda b, e: (b, 0)),
                      pl.BlockSpec((1, D_MODEL, EXPERT_HIDDEN), lambda b, e: (e[b], 0, 0)),
                      pl.BlockSpec((1, D_MODEL, EXPERT_HIDDEN), lambda b, e: (e[b], 0, 0)),
                      pl.BlockSpec((1, EXPERT_HIDDEN, D_MODEL), lambda b, e: (e[b], 0, 0))],
            out_specs=pl.BlockSpec((EXPERT_ROW_BLOCK, D_MODEL), lambda b, e: (b, 0))),
        out_shape=jax.ShapeDtypeStruct((R, D_MODEL), F32),
        compiler_params=_cparams(("arbitrary",)),
        name="moe_experts",
    )(blk_e, buf, w1, w3, w2)


def _combine_body(final, dest_ref, x_ref, wt_ref, gf_ref, yb_ref, o_ref, g0_ref, g1_ref, sem):
    i = pl.program_id(0)
    bufs = (g0_ref, g1_ref)

    def issue(t, carry):
        for s in range(EXPERT_TOPK):
            d = dest_ref[(i * TM + t) * EXPERT_TOPK + s]
            pltpu.make_async_copy(yb_ref.at[pl.ds(d, 1), :], bufs[s].at[pl.ds(t, 1), :], sem).start()
        return carry

    lax.fori_loop(0, TM, issue, 0)
    for s in range(EXPERT_TOPK):
        pltpu.make_async_copy(yb_ref.at[pl.ds(0, TM), :], bufs[s], sem).wait()
    wt = wt_ref[...]
    x = x_ref[...] + (g0_ref[...] * wt[:, 0:1] + g1_ref[...] * wt[:, 1:2])
    if final:
        x = _rms(x, gf_ref[...])
    o_ref[...] = x


def _moe_combine(dest_flat, x1, wts, gf, yb, final):
    T = x1.shape[0]
    nt = T // TM
    return pl.pallas_call(
        functools.partial(_combine_body, final),
        grid_spec=pltpu.PrefetchScalarGridSpec(
            num_scalar_prefetch=1, grid=(nt,),
            in_specs=[pl.BlockSpec((TM, D_MODEL), lambda i, d: (i, 0)),
                      pl.BlockSpec((TM, EXPERT_TOPK), lambda i, d: (i, 0)),
                      pl.BlockSpec((1, D_MODEL), lambda i, d: (0, 0)),
                      pl.BlockSpec(memory_space=pl.ANY)],
            out_specs=pl.BlockSpec((TM, D_MODEL), lambda i, d: (i, 0)),
            scratch_shapes=[pltpu.VMEM((TM, D_MODEL), F32), pltpu.VMEM((TM, D_MODEL), F32),
                            pltpu.SemaphoreType.DMA(())]),
        out_shape=jax.ShapeDtypeStruct((T, D_MODEL), F32),
        compiler_params=_cparams(("arbitrary",)),
        name="moe_combine",
    )(dest_flat, x1, wts, gf, yb)


def _pad_cols(w, n):
    return jnp.pad(w, ((0, 0), (0, n - w.shape[1])))


def _rwkv_cols(w_rw, w_vm):
    W = BRANCH_WIDTH
    o1, o2, o3, o4 = 3 * W, 3 * W + DECAY_LORA, 3 * W + DECAY_LORA + ICLR_LORA, 3 * W + DECAY_LORA + ICLR_LORA + GATE_LORA
    return jnp.concatenate([w_rw[:, :o1], _pad_cols(w_rw[:, o1:o2], LANES), _pad_cols(w_rw[:, o2:o3], LANES),
                            w_rw[:, o3:o4], _pad_cols(w_vm, LANES)], axis=1)


def _pad_rows(w, n):
    return jnp.pad(w, ((0, n - w.shape[0]), (0, 0)))


def kernel(x, norm1_g, w_in, w_vmix, rwkv_mu, vmix_mu, w0, w_decay_up, a0, w_iclr_up, w_gate_up, k_k, k_a, r_k,
           lnx_g, lnx_b, v0, w_vmix_up, pool_w, pool_scale, w_branch, w_out, norm2_g, router_grp_w, router_grp_b,
           router_exp_w, router_exp_b, exp_w1, exp_w3, exp_w2, final_norm_g):
    B, S, D = x.shape
    assert D == D_MODEL and S % TM == 0
    T = B * S
    ns = S // TM
    nb = ns
    xf = x.reshape(T, D)

    slopes = [2.0 ** (-8.0 * (h + 1) / ATTN_HEADS) for h in range(ATTN_HEADS)]
    slope_cols = jnp.zeros((N_PAIR, 2 * TM, LANES), F32)
    for p in range(N_PAIR):
        slope_cols = slope_cols.at[p, :TM, 0].set(slopes[2 * p]).at[p, TM:, 0].set(slopes[2 * p + 1])
    slope_cols = slope_cols.astype(BF16)
    ccol = jnp.zeros((TM, LANES), F32).at[:, 0].set(jnp.arange(TM, dtype=F32)).astype(BF16)

    n_assign = T * EXPERT_TOPK
    R = -(-(n_assign + N_EXPERTS * (EXPERT_ROW_BLOCK - 1)) // EXPERT_ROW_BLOCK) * EXPERT_ROW_BLOCK
    nblk = R // EXPERT_ROW_BLOCK

    v_first = jnp.zeros((T, BRANCH_WIDTH), F32)
    for l in range(DEPTH):
        wl = w_in[l]
        wa = wl[:, :ATTN_COLS].astype(BF16)
        wg = wl[:, GATE_OFF:GATE_OFF + GATE_COLS].astype(BF16)
        wp = wl[:, POOL_OFF:POOL_OFF + BRANCH_WIDTH].astype(BF16)
        has_vmix = l > 0
        w_vm = w_vmix[l - 1] if has_vmix else jnp.zeros((D, VMIX_LORA), F32)
        mu_vm = vmix_mu[l - 1] if has_vmix else jnp.zeros((VMIX_LORA,), F32)
        wr = _rwkv_cols(wl[:, RWKV_OFF:], w_vm).astype(BF16)
        mu = _rwkv_cols(rwkv_mu[l][None, :], mu_vm[None, :])
        q, k, v, gates, zp, zr, kmean = _inproj(xf, norm1_g[l][None, :], wa, wg, wp, wr)

        ids = _moba_select(q, kmean.reshape(B, nb, BRANCH_WIDTH), B, S)
        y_attn = _moba_attn(ids, q, k, v, slope_cols, ccol, B, S)

        par = jnp.stack([w0[l], a0[l], k_k[l], k_a[l], r_k[l].reshape(-1),
                         v0[l - 1] if has_vmix else jnp.zeros((BRANCH_WIDTH,), F32),
                         jnp.zeros((BRANCH_WIDTH,), F32), jnp.zeros((BRANCH_WIDTH,), F32)])
        wvm_up = _pad_rows(w_vmix_up[l - 1], LANES) if has_vmix else jnp.zeros((LANES, BRANCH_WIDTH), F32)
        rp, y0, gm, hm, bonus, rgate, v_cur = _rwkv_prep(
            zr, v_first, mu, par, _pad_rows(w_decay_up[l], LANES), _pad_rows(w_iclr_up[l], LANES), w_gate_up[l],
            wvm_up, has_vmix, B, S)
        if l == 0:
            v_first = v_cur
        y_rwkv = _rwkv_scan(rp, y0, gm, hm, bonus, rgate, jnp.stack([lnx_g[l], lnx_b[l]]), B, S)

        y_pool = _pool(zp, pool_w[l].astype(BF16), pool_scale[l][None, :], B, S)

        rw = jnp.zeros((D, LANES), F32).at[:, :N_GROUPS].set(router_grp_w[l]).at[:, N_EXPERTS:2 * N_EXPERTS].set(router_exp_w[l])
        rw_hi = rw.astype(BF16)
        rw_lo = (rw - rw_hi.astype(F32)).astype(BF16)
        rb = jnp.zeros((1, LANES), F32).at[0, :N_GROUPS].set(router_grp_b[l]).at[0, N_EXPERTS:2 * N_EXPERTS].set(router_exp_b[l])
        x1, h2, eid, wts, cnt = _merge(xf, gates, y_attn, y_rwkv, y_pool, w_branch[l].astype(BF16),
                                       w_out[l].astype(BF16), norm2_g[l][None, :], jnp.stack([rw_hi, rw_lo]), rb)

        dest, pends = _moe_dest(eid, cnt)
        dest_flat = dest.reshape(-1)
        blk_e = jnp.minimum(jnp.searchsorted(pends[0, :N_EXPERTS], jnp.arange(nblk, dtype=jnp.int32) * EXPERT_ROW_BLOCK,
                                             side='right'), N_EXPERTS - 1).astype(jnp.int32)
        buf = _moe_scatter(dest_flat, h2, jnp.zeros((R, D), F32))
        yb = _moe_experts(blk_e, buf, exp_w1[l].astype(BF16), exp_w3[l].astype(BF16), exp_w2[l].astype(BF16))
        xf = _moe_combine(dest_flat, x1, wts, final_norm_g[None, :], yb, l == DEPTH - 1)
    return xf.reshape(B, S, D)
```

```python
import functools

import jax
import jax.numpy as jnp
from jax import lax
from jax.experimental import pallas as pl
from jax.experimental.pallas import tpu as pltpu

F32 = jnp.float32
BF16 = jnp.bfloat16
HIGHEST = lax.Precision.HIGHEST
LOG2E = 1.4426950408889634

D_MODEL = 1024
DEPTH = 2
BRANCH_WIDTH = 512
N_BRANCH = 3
ATTN_HEADS = 8
HEAD_DIM = 64
MOBA_BLOCK = 256
MOBA_TOPK = 3
DECAY_LORA = 64
ICLR_LORA = 64
GATE_LORA = 128
VMIX_LORA = 32
LNX_EPS = 64e-5
POOL_WINDOWS = (2, 4, 8, 16)
POOL_GROUP_DIM = 128
N_GROUPS = 4
EXPERTS_PER_GROUP = 8
N_EXPERTS = 32
EXPERT_TOPK = 2
EXPERT_HIDDEN = 512
EXPERT_ROW_BLOCK = 128
RMS_EPS = 1e-6
ATTN_COLS = 3 * BRANCH_WIDTH
GATE_OFF = ATTN_COLS
GATE_COLS = N_BRANCH * D_MODEL
POOL_OFF = GATE_OFF + GATE_COLS
RWKV_OFF = POOL_OFF + BRANCH_WIDTH

LANES = 128
TM = 256
CHUNK = 64
N_CHUNK = TM // CHUNK
N_PAIR = ATTN_HEADS // 2
RW_COLS = 2048
POOL_HALO = 16
ATTN_GROUP = 4
VMEM_LIMIT = 56 * 1024 * 1024


def _cparams(sem):
    return pltpu.CompilerParams(dimension_semantics=sem, vmem_limit_bytes=VMEM_LIMIT)


def _const_spec(shape):
    nd = len(shape)
    return pl.BlockSpec(shape, lambda *a: (0,) * nd, pipeline_mode=pl.Buffered(1))


def _sigmoid(x):
    return 1.0 / (1.0 + jnp.exp(-x))


def _rms(x, g):
    return x * lax.rsqrt(jnp.mean(x * x, axis=-1, keepdims=True) + RMS_EPS) * g


def _split3(x):
    h1 = x.astype(BF16)
    r1 = x - h1.astype(F32)
    h2 = r1.astype(BF16)
    h3 = (r1 - h2.astype(F32)).astype(BF16)
    return h1, h2, h3


def _dot(a, b):
    return jnp.dot(a, b, preferred_element_type=F32)


def _dot_nt(a, b):
    return lax.dot_general(a, b, (((1,), (1,)), ((), ())), preferred_element_type=F32)


def _dot_tn(a, b):
    return lax.dot_general(a, b, (((0,), (0,)), ((), ())), preferred_element_type=F32)


def _inproj_body(x_ref, g_ref, wa_ref, wvt_ref, wg_ref, wp_ref, wr_ref,
                 q_ref, k_ref, vt_ref, gate_ref, zp_ref, zr_ref, km_ref):
    hb = _rms(x_ref[...], g_ref[...]).astype(BF16)
    za = _dot(hb, wa_ref[...])
    q_ref[...] = (za[:, :BRANCH_WIDTH] * (HEAD_DIM ** -0.5 * LOG2E)).astype(BF16)
    k = za[:, BRANCH_WIDTH:]
    k_ref[...] = k.astype(BF16)
    vt = _dot_nt(wvt_ref[...], hb)
    vrow = lax.broadcasted_iota(jnp.int32, (LANES, 1), 0) < HEAD_DIM
    for p in range(N_PAIR):
        vp = vt[p * LANES:(p + 1) * LANES]
        vt_ref[0, p, 0] = jnp.where(vrow, vp, 1.0).astype(BF16)
        vt_ref[0, p, 1] = jnp.where(vrow, 1.0, vp).astype(BF16)
    km_ref[0] = jnp.mean(k, axis=0, keepdims=True)
    gate_ref[...] = _sigmoid(_dot(hb, wg_ref[...]))
    zp_ref[...] = _dot(hb, wp_ref[...])
    zr_ref[...] = _dot(hb, wr_ref[...])


def _inproj(xf, g, wa, wvt, wg, wp, wr):
    T = xf.shape[0]
    nt = T // TM
    row = lambda w: pl.BlockSpec((TM, w), lambda i: (i, 0))
    return pl.pallas_call(
        _inproj_body,
        grid=(nt,),
        in_specs=[row(D_MODEL), _const_spec((1, D_MODEL)), _const_spec(wa.shape), _const_spec(wvt.shape),
                  _const_spec(wg.shape), _const_spec(wp.shape), _const_spec(wr.shape)],
        out_specs=[row(BRANCH_WIDTH), row(BRANCH_WIDTH),
                   pl.BlockSpec((1, N_PAIR, 2, LANES, TM), lambda i: (i, 0, 0, 0, 0)),
                   row(GATE_COLS), row(BRANCH_WIDTH), row(RW_COLS),
                   pl.BlockSpec((1, 1, BRANCH_WIDTH), lambda i: (i, 0, 0))],
        out_shape=[jax.ShapeDtypeStruct((T, BRANCH_WIDTH), BF16), jax.ShapeDtypeStruct((T, BRANCH_WIDTH), BF16),
                   jax.ShapeDtypeStruct((nt, N_PAIR, 2, LANES, TM), BF16),
                   jax.ShapeDtypeStruct((T, GATE_COLS), F32), jax.ShapeDtypeStruct((T, BRANCH_WIDTH), F32),
                   jax.ShapeDtypeStruct((T, RW_COLS), F32), jax.ShapeDtypeStruct((nt, 1, BRANCH_WIDTH), F32)],
        compiler_params=_cparams(("parallel",)),
        name="inproj",
    )(xf, g, wa, wvt, wg, wp, wr)


def _select_body(q_ref, km_ref, o_ref):
    i = pl.program_id(2)
    q = q_ref[...].astype(F32)
    km = km_ref[0]
    nb = km.shape[0]
    lane = lax.broadcasted_iota(jnp.int32, (1, LANES), 1)
    blk = lax.broadcasted_iota(jnp.int32, (nb, TM), 0)
    rows = []
    for half in range(2):
        in_head = (lane < HEAD_DIM) if half == 0 else (lane >= HEAD_DIM)
        g = lax.dot_general(jnp.where(in_head, km, 0.0), q, (((1,), (1,)), ((), ())),
                            precision=HIGHEST, preferred_element_type=F32)
        g = jnp.where(blk < i, g, -jnp.inf)
        for _ in range(MOBA_TOPK):
            mx = jnp.max(g, axis=0, keepdims=True)
            idx = jnp.min(jnp.where(g == mx, blk, nb), axis=0, keepdims=True)
            rows.append(jnp.where(mx > -jnp.inf, idx, -1))
            g = jnp.where(blk == idx, -jnp.inf, g)
    rows += [jnp.full((1, TM), -1, jnp.int32)] * 2
    o_ref[0] = jnp.concatenate(rows, axis=0)


def _moba_select(q, kmean, B, S):
    ns = S // TM
    nb = kmean.shape[1]
    return pl.pallas_call(
        _select_body,
        grid=(B, N_PAIR, ns),
        in_specs=[pl.BlockSpec((TM, LANES), lambda b, p, i: (b * ns + i, p)),
                  pl.BlockSpec((1, nb, LANES), lambda b, p, i: (b, 0, p))],
        out_specs=pl.BlockSpec((1, 8, TM), lambda b, p, i: (b * N_PAIR + p, 0, i)),
        out_shape=jax.ShapeDtypeStruct((B * N_PAIR, 8, S), jnp.int32),
        compiler_params=_cparams(("parallel", "parallel", "parallel")),
        name="moba_select",
    )(q, kmean)


def _attn_body(ids_ref, q_ref, k_ref, vt_ref, slope_ref, ccol_ref, o_ref, qa_ref, m_ref, acc_ref):
    i = pl.program_id(2)
    lane = lax.broadcasted_iota(jnp.int32, (1, LANES), 1)
    first = lane < HEAD_DIM
    q = q_ref[...]
    zero = jnp.zeros_like(q)
    qa_ref[0, :, 0:LANES] = jnp.where(first, q, zero)
    qa_ref[1, :, 0:LANES] = jnp.where(first, zero, q)
    qa_ref[:, :, LANES:] = slope_ref[0]
    ids = ids_ref[0]
    vrow = lax.broadcasted_iota(jnp.int32, (LANES, 1), 0) < HEAD_DIM

    def scores(j0, nblk):
        kj = k_ref[pl.ds(pl.multiple_of(j0 * TM, TM), nblk * TM), :]
        kaug = jnp.concatenate([kj, ccol_ref[0:nblk * TM, :]], axis=1)
        return [_dot_nt(kaug, qa_ref[h]) for h in range(2)]

    def pv(h, p, j0, nblk):
        vaug = jnp.concatenate([vt_ref[j0 + u, 0, h] for u in range(nblk)], axis=1)
        return _dot(vaug, p)

    kidx = lax.broadcasted_iota(jnp.int32, (TM, TM), 0)
    qidx = lax.broadcasted_iota(jnp.int32, (TM, TM), 1)
    for h, s in enumerate(scores(i, 1)):
        s = jnp.where(kidx <= qidx, s, -jnp.inf)
        m0 = jnp.max(s, axis=0, keepdims=True)
        m_ref[h] = m0
        acc_ref[h] = pv(h, jnp.exp2(s - m0).astype(BF16), i, 1)

    def group_step(j0, nblk):
        dist = ((j0 - i) * TM).astype(F32)
        for h, s in enumerate(scores(j0, nblk)):
            slope = slope_ref[0, h, 0:1, 0:1].astype(F32) + slope_ref[0, h, 0:1, 1:2].astype(F32)
            c0 = slope * dist
            r0 = h * MOBA_TOPK
            sels, best = [], None
            for u in range(nblk):
                j = j0 + u
                sel = (ids[r0:r0 + 1] == j) | (ids[r0 + 1:r0 + 2] == j) | (ids[r0 + 2:r0 + 3] == j)
                cand = jnp.where(sel, jnp.max(s[u * TM:(u + 1) * TM], axis=0, keepdims=True), -jnp.inf)
                best = cand if best is None else jnp.maximum(best, cand)
                sels.append(sel)
            m_old = m_ref[h]
            m_new = jnp.maximum(m_old, best + c0)
            alpha = jnp.exp2(m_old - m_new)
            ps = []
            for u in range(nblk):
                shift = jnp.where(sels[u], m_new - c0, jnp.inf)
                ps.append(jnp.exp2(s[u * TM:(u + 1) * TM] - shift).astype(BF16))
            p = ps[0] if nblk == 1 else jnp.concatenate(ps, axis=0)
            acc_ref[h] = acc_ref[h] * alpha + pv(h, p, j0, nblk)
            m_ref[h] = m_new

    n_group = i // ATTN_GROUP

    def group_body(g, carry):
        group_step(g * ATTN_GROUP, ATTN_GROUP)
        return carry

    def tail_body(j, carry):
        group_step(j, 1)
        return carry

    lax.fori_loop(0, n_group, group_body, 0)
    lax.fori_loop(n_group * ATTN_GROUP, i, tail_body, 0)
    acc_a, acc_b = acc_ref[0], acc_ref[1]
    out_t = jnp.where(vrow, acc_a / acc_a[HEAD_DIM:HEAD_DIM + 1, :], acc_b / acc_b[0:1, :])
    o_ref[...] = out_t.T.astype(BF16)


def _moba_attn(ids, q, k, vt, slope_cols, ccol, B, S):
    ns = S // TM
    return pl.pallas_call(
        _attn_body,
        grid=(B, N_PAIR, ns),
        in_specs=[pl.BlockSpec((1, 8, TM), lambda b, p, i: (b * N_PAIR + p, 0, i)),
                  pl.BlockSpec((TM, LANES), lambda b, p, i: (b * ns + i, p)),
                  pl.BlockSpec((S, LANES), lambda b, p, i: (b, p)),
                  pl.BlockSpec((ns, 1, 2, LANES, TM), lambda b, p, i: (b, p, 0, 0, 0)),
                  pl.BlockSpec((1, 2, TM, LANES), lambda b, p, i: (p, 0, 0, 0)),
                  pl.BlockSpec((ATTN_GROUP * TM, LANES), lambda b, p, i: (0, 0))],
        out_specs=pl.BlockSpec((TM, LANES), lambda b, p, i: (b * ns + i, p)),
        out_shape=jax.ShapeDtypeStruct((B * S, BRANCH_WIDTH), BF16),
        scratch_shapes=[pltpu.VMEM((2, TM, 2 * LANES), BF16), pltpu.VMEM((2, 1, TM), F32),
                        pltpu.VMEM((2, LANES, TM), F32)],
        compiler_params=_cparams(("parallel", "parallel", "arbitrary")),
        name="moba_attn",
    )(ids, q, k, vt, slope_cols, ccol)


def _pool_body(z_ref, halo_ref, w_ref, sc_ref, o_ref, ext_ref):
    i = pl.program_id(1)
    ext_ref[0:POOL_HALO, :] = jnp.where(i == 0, 0.0, halo_ref[...])
    ext_ref[POOL_HALO:, :] = z_ref[...]
    t1 = (i * TM + 1 + lax.broadcasted_iota(jnp.int32, (TM, 1), 0)).astype(F32)
    for gi, win in enumerate(POOL_WINDOWS):
        cols = slice(gi * POOL_GROUP_DIM, (gi + 1) * POOL_GROUP_DIM)
        acc = ext_ref[POOL_HALO:, cols]
        for d in range(1, win):
            acc = acc + ext_ref[POOL_HALO - d:POOL_HALO - d + TM, cols]
        p = acc / jnp.minimum(t1, float(win)) - ext_ref[POOL_HALO:, cols]
        y = _dot(p.astype(BF16), w_ref[gi])
        o_ref[:, cols] = (y * sc_ref[:, cols]).astype(BF16)


def _pool(zp, w, sc, B, S):
    ns = S // TM
    hb = TM // POOL_HALO
    return pl.pallas_call(
        _pool_body,
        grid=(B, ns),
        in_specs=[pl.BlockSpec((TM, BRANCH_WIDTH), lambda b, i: (b * ns + i, 0)),
                  pl.BlockSpec((POOL_HALO, BRANCH_WIDTH), lambda b, i: (jnp.maximum((b * ns + i) * hb - 1, 0), 0)),
                  _const_spec(w.shape), _const_spec(sc.shape)],
        out_specs=pl.BlockSpec((TM, BRANCH_WIDTH), lambda b, i: (b * ns + i, 0)),
        out_shape=jax.ShapeDtypeStruct((B * S, BRANCH_WIDTH), BF16),
        scratch_shapes=[pltpu.VMEM((POOL_HALO + TM, BRANCH_WIDTH), F32)],
        compiler_params=_cparams(("parallel", "parallel")),
        name="pool",
    )(zp, zp, w, sc)


def _seg_sum(x, seg):
    hi = x.astype(BF16)
    lo = (x - hi.astype(F32)).astype(BF16)
    outs = []
    for p in range(N_PAIR):
        cols = slice(p * LANES, (p + 1) * LANES)
        outs.append(_dot(hi[:, cols], seg) + _dot(lo[:, cols], seg))
    return jnp.concatenate(outs, axis=1)


def _rwkv_prep_body(has_vmix, z_ref, halo_ref, vf_ref, mu_ref, par_ref, wd_ref, wa_ref, wg_ref, wvm_ref,
                    rp_ref, y0_ref, g_ref, h_ref, bonus_ref, gate_ref, v_ref):
    i = pl.program_id(1)
    W = BRANCH_WIDTH
    z = z_ref[...]
    prev0 = jnp.where(i == 0, 0.0, halo_ref[7:8, :])
    row = lax.broadcasted_iota(jnp.int32, (TM, 1), 0)
    prev = jnp.where(row == 0, prev0, pltpu.roll(z, 1, axis=0))
    rw = z + (prev - z) * mu_ref[...]
    r, k, v = rw[:, 0:W], rw[:, W:2 * W], rw[:, 2 * W:3 * W]
    w_lo = rw[:, 3 * W:3 * W + LANES]
    a_lo = rw[:, 3 * W + LANES:3 * W + 2 * LANES]
    g_lo = rw[:, 3 * W + 2 * LANES:3 * W + 3 * LANES]
    vm_lo = rw[:, 3 * W + 3 * LANES:3 * W + 4 * LANES]
    w0, a0, k_k, k_a, r_k, v0 = (par_ref[n:n + 1, :] for n in range(6))
    hdot = functools.partial(jnp.dot, precision=HIGHEST, preferred_element_type=F32)

    u = w0 + hdot(jnp.tanh(w_lo), wd_ref[...])
    nu = -u
    softplus = jnp.maximum(nu, 0.0) + jnp.log(1.0 + jnp.exp(-jnp.abs(nu)))
    logw = -jnp.exp(-softplus - 0.5)
    if has_vmix:
        v = v + (vf_ref[...] - v) * _sigmoid(v0 + hdot(vm_lo, wvm_ref[...]))
    v_ref[...] = v
    a = _sigmoid(a0 + hdot(a_lo, wa_ref[...]))
    gate_ref[...] = hdot(_sigmoid(g_lo), wg_ref[...])

    lane = lax.broadcasted_iota(jnp.int32, (1, LANES), 1)
    first = lane < HEAD_DIM
    seg = ((lax.broadcasted_iota(jnp.int32, (LANES, LANES), 0) < HEAD_DIM)
           == (lax.broadcasted_iota(jnp.int32, (LANES, LANES), 1) < HEAD_DIM)).astype(BF16)
    kk = k * k_k
    kk = kk / jnp.maximum(jnp.sqrt(_seg_sum(kk * kk, seg)), 1e-12)
    k2 = k * (1.0 + (a - 1.0) * k_a)
    bonus_ref[...] = _seg_sum(r * k2 * r_k, seg) * v

    ri = lax.broadcasted_iota(jnp.int32, (TM, TM), 0)
    ci = lax.broadcasted_iota(jnp.int32, (TM, TM), 1)
    same = (ri // CHUNK) == (ci // CHUNK)
    incl = same & (ri >= ci)
    strict = same & (ri > ci)
    lmat = incl.astype(BF16)
    h1, h2, h3 = _split3(logw)
    cw = _dot(lmat, h1) + _dot(lmat, h2) + _dot(lmat, h3)
    e_neg = jnp.exp(-cw)
    at = (-kk) * jnp.exp(cw - logw)
    bt = (kk * a) * e_neg
    kt = k2 * e_neg
    rt = r * jnp.exp(cw)
    eye = (ri == ci).astype(F32)
    bd = ((lax.broadcasted_iota(jnp.int32, (LANES, LANES), 0) < HEAD_DIM)
          == (lax.broadcasted_iota(jnp.int32, (LANES, LANES), 1) < HEAD_DIM))
    eye_l = (lax.broadcasted_iota(jnp.int32, (LANES, LANES), 0)
             == lax.broadcasted_iota(jnp.int32, (LANES, LANES), 1)).astype(F32)

    for p in range(N_PAIR):
        cols = slice(p * LANES, (p + 1) * LANES)
        ap, bp, kp, rp, vp = at[:, cols], bt[:, cols], kt[:, cols], rt[:, cols], v[:, cols]
        bk = jnp.concatenate([bp, kp], axis=0).astype(BF16)
        p_pair = jnp.zeros((TM, LANES), F32)
        q_pair = jnp.zeros((TM, LANES), F32)
        rp_pair = jnp.zeros((TM, LANES), F32)
        y0_pair = jnp.zeros((TM, LANES), F32)
        for half in range(2):
            mh = first if half == 0 else jnp.logical_not(first)
            am = jnp.where(mh, ap, 0.0)
            rm = jnp.where(mh, rp, 0.0)
            vm = jnp.where(mh, vp, 0.0).astype(BF16)
            big = _dot_nt(jnp.concatenate([am, rm], axis=0).astype(BF16), bk)
            n = jnp.where(strict, big[:TM, :TM], 0.0)
            aak = jnp.where(strict, big[:TM, TM:], 0.0)
            mrb = jnp.where(incl, big[TM:, :TM], 0.0)
            mrk = jnp.where(incl, big[TM:, TM:], 0.0)
            tm = eye + n
            npow = n
            steps = CHUNK.bit_length() - 2
            for _ in range(steps):
                nb16 = npow.astype(BF16)
                npow = _dot(nb16, nb16)
                tm = tm + _dot(tm.astype(BF16), npow.astype(BF16))
            av = _dot(aak.astype(BF16), vm)
            x = _dot(tm.astype(BF16), jnp.concatenate([am, av], axis=1).astype(BF16))
            ph, qh = x[:, :LANES], x[:, LANES:]
            mrb16 = mrb.astype(BF16)
            p_pair = p_pair + ph
            q_pair = q_pair + qh
            rp_pair = rp_pair + rm + _dot(mrb16, ph.astype(BF16))
            y0_pair = y0_pair + _dot(mrb16, qh.astype(BF16)) + _dot(mrk.astype(BF16), vm)
        rp_ref[:, cols] = rp_pair
        y0_ref[:, cols] = y0_pair
        for c in range(N_CHUNK):
            rows = slice(c * CHUNK, (c + 1) * CHUNK)
            wc = jnp.exp(cw[(c + 1) * CHUNK - 1:(c + 1) * CHUNK, cols])
            bc = bp[rows].astype(BF16)
            gp = _dot_tn(p_pair[rows].astype(BF16), bc)
            hp = _dot_tn(jnp.concatenate([q_pair[rows], vp[rows]], axis=0).astype(BF16),
                         jnp.concatenate([bp[rows], kp[rows]], axis=0).astype(BF16))
            g_ref[0, c, p] = (eye_l + jnp.where(bd, gp, 0.0)) * wc
            h_ref[0, c, p] = jnp.where(bd, hp, 0.0) * wc


def _rwkv_prep(zr, vfirst, mu, par, wd, wa, wg, wvm, has_vmix, B, S):
    ns = S // TM
    T = B * S
    tile = lambda w: pl.BlockSpec((TM, w), lambda b, i: (b * ns + i, 0))
    gh_spec = pl.BlockSpec((1, N_CHUNK, N_PAIR, LANES, LANES), lambda b, i: (b * ns + i, 0, 0, 0, 0))
    gh_shape = jax.ShapeDtypeStruct((B * ns, N_CHUNK, N_PAIR, LANES, LANES), F32)
    wide = jax.ShapeDtypeStruct((T, BRANCH_WIDTH), F32)
    return pl.pallas_call(
        functools.partial(_rwkv_prep_body, has_vmix),
        grid=(B, ns),
        in_specs=[tile(RW_COLS),
                  pl.BlockSpec((8, RW_COLS), lambda b, i: (jnp.maximum((b * ns + i) * (TM // 8) - 1, 0), 0)),
                  tile(BRANCH_WIDTH), _const_spec(mu.shape), _const_spec(par.shape), _const_spec(wd.shape),
                  _const_spec(wa.shape), _const_spec(wg.shape), _const_spec(wvm.shape)],
        out_specs=[tile(BRANCH_WIDTH), tile(BRANCH_WIDTH), gh_spec, gh_spec,
                   tile(BRANCH_WIDTH), tile(BRANCH_WIDTH), tile(BRANCH_WIDTH)],
        out_shape=[wide, wide, gh_shape, gh_shape, wide, wide, wide],
        compiler_params=_cparams(("parallel", "parallel")),
        name="rwkv_prep",
    )(zr, zr, vfirst, mu, par, wd, wa, wg, wvm)


def _rwkv_scan_body(rp_ref, y0_ref, g_ref, h_ref, bonus_ref, gate_ref, ln_ref, o_ref, s_ref):
    i = pl.program_id(1)

    @pl.when(i == 0)
    def _():
        s_ref[...] = jnp.zeros_like(s_ref)

    ys = []
    for p in range(N_PAIR):
        cols = slice(p * LANES, (p + 1) * LANES)
        s = s_ref[p]
        yc = []
        for c in range(N_CHUNK):
            rows = slice(c * CHUNK, (c + 1) * CHUNK)
            s16 = s.astype(BF16)
            yc.append(_dot_nt(rp_ref[rows, cols].astype(BF16), s16) + y0_ref[rows, cols])
            s = _dot(s16, g_ref[0, c, p].astype(BF16)) + h_ref[0, c, p]
        s_ref[p] = s
        ys.append(jnp.concatenate(yc, axis=0))
    y = jnp.concatenate(ys, axis=1)
    seg = ((lax.broadcasted_iota(jnp.int32, (LANES, LANES), 0) < HEAD_DIM)
           == (lax.broadcasted_iota(jnp.int32, (LANES, LANES), 1) < HEAD_DIM)).astype(BF16)
    mu = _seg_sum(y, seg) * (1.0 / HEAD_DIM)
    d = y - mu
    var = _seg_sum(d * d, seg) * (1.0 / HEAD_DIM)
    yn = d * lax.rsqrt(var + LNX_EPS) * ln_ref[0:1, :] + ln_ref[1:2, :]
    o_ref[...] = ((yn + bonus_ref[...]) * gate_ref[...]).astype(BF16)


def _rwkv_scan(rp, y0, g, h, bonus, gate, ln, B, S):
    ns = S // TM
    tile = pl.BlockSpec((TM, BRANCH_WIDTH), lambda b, i: (b * ns + i, 0))
    gh_spec = pl.BlockSpec((1, N_CHUNK, N_PAIR, LANES, LANES), lambda b, i: (b * ns + i, 0, 0, 0, 0))
    return pl.pallas_call(
        _rwkv_scan_body,
        grid=(B, ns),
        in_specs=[tile, tile, gh_spec, gh_spec, tile, tile, _const_spec(ln.shape)],
        out_specs=tile,
        out_shape=jax.ShapeDtypeStruct((B * S, BRANCH_WIDTH), BF16),
        scratch_shapes=[pltpu.VMEM((N_PAIR, LANES, LANES), F32)],
        compiler_params=_cparams(("parallel", "arbitrary")),
        name="rwkv_scan",
    )(rp, y0, g, h, bonus, gate, ln)


def _merge_body(x_ref, gate_ref, ya_ref, yr_ref, yp_ref, wb_ref, wo_ref, g2_ref, rw_ref, rb_ref,
                x1_ref, h2_ref, eid_ref, wt_ref, cnt_ref):
    merged = jnp.zeros((TM, D_MODEL), F32)
    for bi, y_ref in enumerate((ya_ref, yr_ref, yp_ref)):
        merged = merged + gate_ref[:, bi * D_MODEL:(bi + 1) * D_MODEL] * _dot(y_ref[...], wb_ref[bi])
    x1 = x_ref[...] + _dot(merged.astype(BF16), wo_ref[...])
    x1_ref[...] = x1
    h2 = _rms(x1, g2_ref[...])
    h2_ref[...] = h2
    hi = h2.astype(BF16)
    lo = (h2 - hi.astype(F32)).astype(BF16)
    logits = _dot(hi, rw_ref[0]) + _dot(lo, rw_ref[0]) + _dot(hi, rw_ref[1]) + rb_ref[...]
    lane = lax.broadcasted_iota(jnp.int32, (TM, LANES), 1)
    ninf = -jnp.inf
    glog = jnp.where(lane < N_GROUPS, logits, ninf)
    gmax = jnp.max(glog, axis=-1, keepdims=True)
    grp = jnp.min(jnp.where(glog == gmax, lane, LANES), axis=-1, keepdims=True)
    gprob = 1.0 / jnp.sum(jnp.exp(glog - gmax), axis=-1, keepdims=True)
    e_lane = lane - N_EXPERTS
    in_grp = (e_lane >= grp * EXPERTS_PER_GROUP) & (e_lane < (grp + 1) * EXPERTS_PER_GROUP)
    elog = jnp.where(in_grp, logits, ninf)
    v1 = jnp.max(elog, axis=-1, keepdims=True)
    i1 = jnp.min(jnp.where(elog == v1, e_lane, LANES), axis=-1, keepdims=True)
    elog = jnp.where(e_lane == i1, ninf, elog)
    v2 = jnp.max(elog, axis=-1, keepdims=True)
    i2 = jnp.min(jnp.where(elog == v2, e_lane, LANES), axis=-1, keepdims=True)
    e2 = jnp.exp(v2 - v1)
    den = 1.0 + e2
    eid_ref[...] = jnp.concatenate([i1, i2], axis=1)
    wt_ref[...] = jnp.concatenate([gprob / den, gprob * e2 / den], axis=1)
    hit = ((lane == i1) | (lane == i2)).astype(F32)
    cnt_ref[0] = jnp.sum(hit, axis=0, keepdims=True).astype(jnp.int32)


def _merge(xf, gates, ya, yr, yp, wb, wo, g2, rw, rb):
    T = xf.shape[0]
    nt = T // TM
    row = lambda w: pl.BlockSpec((TM, w), lambda i: (i, 0))
    return pl.pallas_call(
        _merge_body,
        grid=(nt,),
        in_specs=[row(D_MODEL), row(GATE_COLS), row(BRANCH_WIDTH), row(BRANCH_WIDTH), row(BRANCH_WIDTH),
                  _const_spec(wb.shape), _const_spec(wo.shape), _const_spec(g2.shape), _const_spec(rw.shape),
                  _const_spec(rb.shape)],
        out_specs=[row(D_MODEL), row(D_MODEL), row(EXPERT_TOPK), row(EXPERT_TOPK),
                   pl.BlockSpec((1, 1, LANES), lambda i: (i, 0, 0))],
        out_shape=[jax.ShapeDtypeStruct((T, D_MODEL), F32), jax.ShapeDtypeStruct((T, D_MODEL), F32),
                   jax.ShapeDtypeStruct((T, EXPERT_TOPK), jnp.int32), jax.ShapeDtypeStruct((T, EXPERT_TOPK), F32),
                   jax.ShapeDtypeStruct((nt, 1, LANES), jnp.int32)],
        compiler_params=_cparams(("parallel",)),
        name="merge",
    )(xf, gates, ya, yr, yp, wb, wo, g2, rw, rb)


def _lane_excl_cumsum(x):
    lane = lax.broadcasted_iota(jnp.int32, x.shape, 1)
    inc = x
    sh = 1
    while sh < LANES:
        inc = inc + jnp.where(lane >= sh, pltpu.roll(inc, sh, axis=1), 0)
        sh *= 2
    return inc - x


def _dest_body(eid_ref, cnt_ref, dest_ref, pend_ref, base_ref):
    i = pl.program_id(0)

    @pl.when(i == 0)
    def _():
        total = jnp.sum(cnt_ref[...], axis=0)
        shift = EXPERT_ROW_BLOCK.bit_length() - 1
        padded = jnp.left_shift(jnp.right_shift(total + (EXPERT_ROW_BLOCK - 1), shift), shift)
        pstart = _lane_excl_cumsum(padded)
        base_ref[...] = pstart
        pend_ref[...] = pstart + padded

    lane = lax.broadcasted_iota(jnp.int32, (TM, LANES), 1)
    e0, e1 = eid_ref[:, 0:1], eid_ref[:, 1:2]
    oh0, oh1 = lane == e0, lane == e1
    both = (oh0 | oh1).astype(BF16)
    lower = (lax.broadcasted_iota(jnp.int32, (TM, TM), 0)
             > lax.broadcasted_iota(jnp.int32, (TM, TM), 1)).astype(BF16)
    pos = _dot(lower, both) + base_ref[...].astype(F32)
    d0 = jnp.sum(jnp.where(oh0, pos, 0.0), axis=-1, keepdims=True)
    d1 = jnp.sum(jnp.where(oh1, pos, 0.0), axis=-1, keepdims=True)
    dest_ref[...] = jnp.concatenate([d0, d1], axis=1).astype(jnp.int32)
    base_ref[...] = base_ref[...] + jnp.sum(both.astype(F32), axis=0, keepdims=True).astype(jnp.int32)


def _moe_dest(eid, cnt):
    T = eid.shape[0]
    nt = T // TM
    return pl.pallas_call(
        _dest_body,
        grid=(nt,),
        in_specs=[pl.BlockSpec((TM, EXPERT_TOPK), lambda i: (i, 0)),
                  pl.BlockSpec((nt, 1, LANES), lambda i: (0, 0, 0))],
        out_specs=[pl.BlockSpec((TM, EXPERT_TOPK), lambda i: (i, 0)),
                   pl.BlockSpec((1, LANES), lambda i: (0, 0))],
        out_shape=[jax.ShapeDtypeStruct((T, EXPERT_TOPK), jnp.int32), jax.ShapeDtypeStruct((1, LANES), jnp.int32)],
        scratch_shapes=[pltpu.VMEM((1, LANES), jnp.int32)],
        compiler_params=_cparams(("arbitrary",)),
        name="moe_dest",
    )(eid, cnt)


def _scatter_body(dest_ref, h_ref, buf_in_ref, buf_ref, sem):
    del buf_in_ref
    i = pl.program_id(0)

    def issue(t, carry):
        for s in range(EXPERT_TOPK):
            d = dest_ref[(i * TM + t) * EXPERT_TOPK + s]
            pltpu.make_async_copy(h_ref.at[pl.ds(t, 1), :], buf_ref.at[pl.ds(d, 1), :], sem).start()
        return carry

    lax.fori_loop(0, TM, issue, 0)
    for _ in range(EXPERT_TOPK):
        pltpu.make_async_copy(h_ref, buf_ref.at[pl.ds(0, TM), :], sem).wait()


def _moe_scatter(dest_flat, h2, buf0):
    T = h2.shape[0]
    nt = T // TM
    return pl.pallas_call(
        _scatter_body,
        grid_spec=pltpu.PrefetchScalarGridSpec(
            num_scalar_prefetch=1, grid=(nt,),
            in_specs=[pl.BlockSpec((TM, D_MODEL), lambda i, d: (i, 0)), pl.BlockSpec(memory_space=pl.ANY)],
            out_specs=pl.BlockSpec(memory_space=pl.ANY),
            scratch_shapes=[pltpu.SemaphoreType.DMA(())]),
        out_shape=jax.ShapeDtypeStruct(buf0.shape, F32),
        input_output_aliases={2: 0},
        compiler_params=_cparams(("arbitrary",)),
        name="moe_scatter",
    )(dest_flat, h2, buf0)


def _expert_body(be_ref, x_ref, w1_ref, w3_ref, w2_ref, o_ref):
    del be_ref
    xb = x_ref[...].astype(BF16)
    h1 = _dot(xb, w1_ref[0])
    h3 = _dot(xb, w3_ref[0])
    hid = (h1 * _sigmoid(h1)) * h3
    o_ref[...] = _dot(hid.astype(BF16), w2_ref[0])


def _moe_experts(blk_e, buf, w1, w3, w2):
    R = buf.shape[0]
    nblk = R // EXPERT_ROW_BLOCK
    return pl.pallas_call(
        _expert_body,
        grid_spec=pltpu.PrefetchScalarGridSpec(
            num_scalar_prefetch=1, grid=(nblk,),
            in_specs=[pl.BlockSpec((EXPERT_ROW_BLOCK, D_MODEL), lambda b, e: (b, 0)),
                      pl.BlockSpec((1, D_MODEL, EXPERT_HIDDEN), lambda b, e: (e[b], 0, 0)),
                      pl.BlockSpec((1, D_MODEL, EXPERT_HIDDEN), lambda b, e: (e[b], 0, 0)),
                      pl.BlockSpec((1, EXPERT_HIDDEN, D_MODEL), lambda b, e: (e[b], 0, 0))],
            out_specs=pl.BlockSpec((EXPERT_ROW_BLOCK, D_MODEL), lambda b, e: (b, 0))),
        out_shape=jax.ShapeDtypeStruct((R, D_MODEL), F32),
        compiler_params=_cparams(("arbitrary",)),
        name="moe_experts",
    )(blk_e, buf, w1, w3, w2)


def _combine_body(final, dest_ref, x_ref, wt_ref, gf_ref, yb_ref, o_ref, g0_ref, g1_ref, sem):
    i = pl.program_id(0)
    bufs = (g0_ref, g1_ref)

    def issue(t, carry):
        for s in range(EXPERT_TOPK):
            d = dest_ref[(i * TM + t) * EXPERT_TOPK + s]
            pltpu.make_async_copy(yb_ref.at[pl.ds(d, 1), :], bufs[s].at[pl.ds(t, 1), :], sem).start()
        return carry

    lax.fori_loop(0, TM, issue, 0)
    for s in range(EXPERT_TOPK):
        pltpu.make_async_copy(yb_ref.at[pl.ds(0, TM), :], bufs[s], sem).wait()
    wt = wt_ref[...]
    x = x_ref[...] + (g0_ref[...] * wt[:, 0:1] + g1_ref[...] * wt[:, 1:2])
    if final:
        x = _rms(x, gf_ref[...])
    o_ref[...] = x


def _moe_combine(dest_flat, x1, wts, gf, yb, final):
    T = x1.shape[0]
    nt = T // TM
    return pl.pallas_call(
        functools.partial(_combine_body, final),
        grid_spec=pltpu.PrefetchScalarGridSpec(
            num_scalar_prefetch=1, grid=(nt,),
            in_specs=[pl.BlockSpec((TM, D_MODEL), lambda i, d: (i, 0)),
                      pl.BlockSpec((TM, EXPERT_TOPK), lambda i, d: (i, 0)),
                      pl.BlockSpec((1, D_MODEL), lambda i, d: (0, 0)),
                      pl.BlockSpec(memory_space=pl.ANY)],
            out_specs=pl.BlockSpec((TM, D_MODEL), lambda i, d: (i, 0)),
            scratch_shapes=[pltpu.VMEM((TM, D_MODEL), F32), pltpu.VMEM((TM, D_MODEL), F32),
                            pltpu.SemaphoreType.DMA(())]),
        out_shape=jax.ShapeDtypeStruct((T, D_MODEL), F32),
        compiler_params=_cparams(("arbitrary",)),
        name="moe_combine",
    )(dest_flat, x1, wts, gf, yb)


def _pad_cols(w, n):
    return jnp.pad(w, ((0, 0), (0, n - w.shape[1])))


def _rwkv_cols(w_rw, w_vm):
    W = BRANCH_WIDTH
    o1, o2, o3, o4 = 3 * W, 3 * W + DECAY_LORA, 3 * W + DECAY_LORA + ICLR_LORA, 3 * W + DECAY_LORA + ICLR_LORA + GATE_LORA
    return jnp.concatenate([w_rw[:, :o1], _pad_cols(w_rw[:, o1:o2], LANES), _pad_cols(w_rw[:, o2:o3], LANES),
                            w_rw[:, o3:o4], _pad_cols(w_vm, LANES)], axis=1)


def _pad_rows(w, n):
    return jnp.pad(w, ((0, n - w.shape[0]), (0, 0)))


def kernel(x, norm1_g, w_in, w_vmix, rwkv_mu, vmix_mu, w0, w_decay_up, a0, w_iclr_up, w_gate_up, k_k, k_a, r_k,
           lnx_g, lnx_b, v0, w_vmix_up, pool_w, pool_scale, w_branch, w_out, norm2_g, router_grp_w, router_grp_b,
           router_exp_w, router_exp_b, exp_w1, exp_w3, exp_w2, final_norm_g):
    B, S, D = x.shape
    assert D == D_MODEL and S % TM == 0
    T = B * S
    ns = S // TM
    nb = ns
    xf = x.reshape(T, D)

    slopes = jnp.asarray([2.0 ** (-8.0 * (h + 1) / ATTN_HEADS) * LOG2E for h in range(ATTN_HEADS)], F32)
    slopes = slopes.reshape(N_PAIR, 2, 1)
    s_hi = slopes.astype(BF16).astype(F32)
    s_lo = slopes - s_hi
    slope_cols = jnp.zeros((N_PAIR, 2, TM, LANES), F32)
    for col, val in enumerate((s_hi, s_lo, s_hi * TM, s_lo * TM)):
        slope_cols = slope_cols.at[:, :, :, col].set(val)
    slope_cols = slope_cols.astype(BF16)
    key_row = jnp.arange(ATTN_GROUP * TM, dtype=jnp.int32)
    in_blk, blk_no = (key_row % TM).astype(F32), (key_row // TM).astype(F32)
    ccol = jnp.zeros((ATTN_GROUP * TM, LANES), F32)
    for col, val in enumerate((in_blk, in_blk, blk_no, blk_no)):
        ccol = ccol.at[:, col].set(val)
    ccol = ccol.astype(BF16)

    n_assign = T * EXPERT_TOPK
    R = -(-(n_assign + N_EXPERTS * (EXPERT_ROW_BLOCK - 1)) // EXPERT_ROW_BLOCK) * EXPERT_ROW_BLOCK
    nblk = R // EXPERT_ROW_BLOCK

    v_first = jnp.zeros((T, BRANCH_WIDTH), F32)
    for l in range(DEPTH):
        wl = w_in[l]
        wa = wl[:, :2 * BRANCH_WIDTH].astype(BF16)
        wvt = wl[:, 2 * BRANCH_WIDTH:ATTN_COLS].T.astype(BF16)
        wg =wl[:, GATE_OFF:GATE_OFF + GATE_COLS].astype(BF16)
        wp = wl[:, POOL_OFF:POOL_OFF + BRANCH_WIDTH].astype(BF16)
        has_vmix = l > 0
        w_vm = w_vmix[l - 1] if has_vmix else jnp.zeros((D, VMIX_LORA), F32)
        mu_vm = vmix_mu[l - 1] if has_vmix else jnp.zeros((VMIX_LORA,), F32)
        wr = _rwkv_cols(wl[:, RWKV_OFF:], w_vm).astype(BF16)
        mu = _rwkv_cols(rwkv_mu[l][None, :], mu_vm[None, :])
        q, k, vt, gates, zp, zr, kmean = _inproj(xf, norm1_g[l][None, :], wa, wvt, wg, wp, wr)

        ids = _moba_select(q, kmean.reshape(B, nb, BRANCH_WIDTH), B, S)
        y_attn = _moba_attn(ids, q, k, vt, slope_cols, ccol, B, S)

        par = jnp.stack([w0[l], a0[l], k_k[l], k_a[l], r_k[l].reshape(-1),
                         v0[l - 1] if has_vmix else jnp.zeros((BRANCH_WIDTH,), F32),
                         jnp.zeros((BRANCH_WIDTH,), F32), jnp.zeros((BRANCH_WIDTH,), F32)])
        wvm_up = _pad_rows(w_vmix_up[l - 1], LANES) if has_vmix else jnp.zeros((LANES, BRANCH_WIDTH), F32)
        rp, y0, gm, hm, bonus, rgate, v_cur = _rwkv_prep(
            zr, v_first, mu, par, _pad_rows(w_decay_up[l], LANES), _pad_rows(w_iclr_up[l], LANES), w_gate_up[l],
            wvm_up, has_vmix, B, S)
        if l == 0:
            v_first = v_cur
        y_rwkv = _rwkv_scan(rp, y0, gm, hm, bonus, rgate, jnp.stack([lnx_g[l], lnx_b[l]]), B, S)

        y_pool = _pool(zp, pool_w[l].astype(BF16), pool_scale[l][None, :], B, S)

        rw = jnp.zeros((D, LANES), F32).at[:, :N_GROUPS].set(router_grp_w[l]).at[:, N_EXPERTS:2 * N_EXPERTS].set(router_exp_w[l])
        rw_hi = rw.astype(BF16)
        rw_lo = (rw - rw_hi.astype(F32)).astype(BF16)
        rb = jnp.zeros((1, LANES), F32).at[0, :N_GROUPS].set(router_grp_b[l]).at[0, N_EXPERTS:2 * N_EXPERTS].set(router_exp_b[l])
        x1, h2, eid, wts, cnt = _merge(xf, gates, y_attn, y_rwkv, y_pool, w_branch[l].astype(BF16),
                                       w_out[l].astype(BF16), norm2_g[l][None, :], jnp.stack([rw_hi, rw_lo]), rb)

        dest, pends = _moe_dest(eid, cnt)
        dest_flat = dest.reshape(-1)
        blk_row = jnp.arange(nblk, dtype=jnp.int32)[:, None] * EXPERT_ROW_BLOCK
        blk_e = jnp.minimum(jnp.sum((pends[0, :N_EXPERTS][None, :] <= blk_row).astype(jnp.int32), axis=1),
                            N_EXPERTS - 1)
        buf = _moe_scatter(dest_flat, h2, jnp.zeros((R, D), F32))
        yb = _moe_experts(blk_e, buf, exp_w1[l].astype(BF16), exp_w3[l].astype(BF16), exp_w2[l].astype(BF16))
        xf = _moe_combine(dest_flat, x1, wts, final_norm_g[None, :], yb, l == DEPTH - 1)
    return xf.reshape(B, S, D)
```

```python
import functools

import jax
import jax.numpy as jnp
from jax import lax
from jax.experimental import pallas as pl
from jax.experimental.pallas import tpu as pltpu

F32 = jnp.float32
BF16 = jnp.bfloat16
HIGHEST = lax.Precision.HIGHEST
LOG2E = 1.4426950408889634

D_MODEL = 1024
DEPTH = 2
BRANCH_WIDTH = 512
N_BRANCH = 3
ATTN_HEADS = 8
HEAD_DIM = 64
MOBA_BLOCK = 256
MOBA_TOPK = 3
DECAY_LORA = 64
ICLR_LORA = 64
GATE_LORA = 128
VMIX_LORA = 32
LNX_EPS = 64e-5
POOL_WINDOWS = (2, 4, 8, 16)
POOL_GROUP_DIM = 128
N_GROUPS = 4
EXPERTS_PER_GROUP = 8
N_EXPERTS = 32
EXPERT_TOPK = 2
EXPERT_HIDDEN = 512
EXPERT_ROW_BLOCK = 128
RMS_EPS = 1e-6
ATTN_COLS = 3 * BRANCH_WIDTH
GATE_OFF = ATTN_COLS
GATE_COLS = N_BRANCH * D_MODEL
POOL_OFF = GATE_OFF + GATE_COLS
RWKV_OFF = POOL_OFF + BRANCH_WIDTH

LANES = 128
TM = 256
CHUNK = 64
N_CHUNK = TM // CHUNK
N_PAIR = ATTN_HEADS // 2
RW_COLS = 2048
POOL_HALO = 16
ATTN_GROUP = 4
VMEM_LIMIT = 56 * 1024 * 1024


def _cparams(sem):
    return pltpu.CompilerParams(dimension_semantics=sem, vmem_limit_bytes=VMEM_LIMIT)


def _const_spec(shape):
    nd = len(shape)
    return pl.BlockSpec(shape, lambda *a: (0,) * nd, pipeline_mode=pl.Buffered(1))


def _sigmoid(x):
    return 1.0 / (1.0 + jnp.exp(-x))


def _rms(x, g):
    return x * lax.rsqrt(jnp.mean(x * x, axis=-1, keepdims=True) + RMS_EPS) * g


def _split3(x):
    h1 = x.astype(BF16)
    r1 = x - h1.astype(F32)
    h2 = r1.astype(BF16)
    h3 = (r1 - h2.astype(F32)).astype(BF16)
    return h1, h2, h3


def _dot(a, b):
    return jnp.dot(a, b, preferred_element_type=F32)


def _dot_nt(a, b):
    return lax.dot_general(a, b, (((1,), (1,)), ((), ())), preferred_element_type=F32)


def _dot_tn(a, b):
    return lax.dot_general(a, b, (((0,), (0,)), ((), ())), preferred_element_type=F32)


def _inproj_body(x_ref, g_ref, wa_ref, wvt_ref, wg_ref, wp_ref, wr_ref,
                 q_ref, k_ref, vt_ref, gate_ref, zp_ref, zr_ref, km_ref):
    hb = _rms(x_ref[...], g_ref[...]).astype(BF16)
    za = _dot(hb, wa_ref[...])
    q_ref[...] = (za[:, :BRANCH_WIDTH] * (HEAD_DIM ** -0.5 * LOG2E)).astype(BF16)
    k = za[:, BRANCH_WIDTH:]
    k_ref[...] = k.astype(BF16)
    vt = _dot_nt(wvt_ref[...], hb)
    vrow = lax.broadcasted_iota(jnp.int32, (LANES, 1), 0) < HEAD_DIM
    for p in range(N_PAIR):
        vp = vt[p * LANES:(p + 1) * LANES]
        vt_ref[0, p, 0] = jnp.where(vrow, vp, 1.0).astype(BF16)
        vt_ref[0, p, 1] = jnp.where(vrow, 1.0, vp).astype(BF16)
    km_ref[0] = jnp.mean(k, axis=0, keepdims=True)
    gate_ref[...] = _sigmoid(_dot(hb, wg_ref[...]))
    zp_ref[...] = _dot(hb, wp_ref[...])
    zr_ref[...] = _dot(hb, wr_ref[...])


def _inproj(xf, g, wa, wvt, wg, wp, wr):
    T = xf.shape[0]
    nt = T // TM
    row = lambda w: pl.BlockSpec((TM, w), lambda i: (i, 0))
    return pl.pallas_call(
        _inproj_body,
        grid=(nt,),
        in_specs=[row(D_MODEL), _const_spec((1, D_MODEL)), _const_spec(wa.shape), _const_spec(wvt.shape),
                  _const_spec(wg.shape), _const_spec(wp.shape), _const_spec(wr.shape)],
        out_specs=[row(BRANCH_WIDTH), row(BRANCH_WIDTH),
                   pl.BlockSpec((1, N_PAIR, 2, LANES, TM), lambda i: (i, 0, 0, 0, 0)),
                   row(GATE_COLS), row(BRANCH_WIDTH), row(RW_COLS),
                   pl.BlockSpec((1, 1, BRANCH_WIDTH), lambda i: (i, 0, 0))],
        out_shape=[jax.ShapeDtypeStruct((T, BRANCH_WIDTH), BF16), jax.ShapeDtypeStruct((T, BRANCH_WIDTH), BF16),
                   jax.ShapeDtypeStruct((nt, N_PAIR, 2, LANES, TM), BF16),
                   jax.ShapeDtypeStruct((T, GATE_COLS), F32), jax.ShapeDtypeStruct((T, BRANCH_WIDTH), F32),
                   jax.ShapeDtypeStruct((T, RW_COLS), F32), jax.ShapeDtypeStruct((nt, 1, BRANCH_WIDTH), F32)],
        compiler_params=_cparams(("parallel",)),
        name="inproj",
    )(xf, g, wa, wvt, wg, wp, wr)


def _select_body(q_ref, km_ref, o_ref):
    i = pl.program_id(2)
    q = q_ref[...].astype(F32)
    km = km_ref[0]
    nb = km.shape[0]
    lane = lax.broadcasted_iota(jnp.int32, (1, LANES), 1)
    blk = lax.broadcasted_iota(jnp.int32, (nb, TM), 0)
    rows = []
    for half in range(2):
        in_head = (lane < HEAD_DIM) if half == 0 else (lane >= HEAD_DIM)
        g = lax.dot_general(jnp.where(in_head, km, 0.0), q, (((1,), (1,)), ((), ())),
                            precision=HIGHEST, preferred_element_type=F32)
        g = jnp.where(blk < i, g, -jnp.inf)
        for _ in range(MOBA_TOPK):
            mx = jnp.max(g, axis=0, keepdims=True)
            idx = jnp.min(jnp.where(g == mx, blk, nb), axis=0, keepdims=True)
            rows.append(jnp.where(mx > -jnp.inf, idx, -1))
            g = jnp.where(blk == idx, -jnp.inf, g)
    rows += [jnp.full((1, TM), -1, jnp.int32)] * 2
    o_ref[0] = jnp.concatenate(rows, axis=0)


def _moba_select(q, kmean, B, S):
    ns = S // TM
    nb = kmean.shape[1]
    return pl.pallas_call(
        _select_body,
        grid=(B, N_PAIR, ns),
        in_specs=[pl.BlockSpec((TM, LANES), lambda b, p, i: (b * ns + i, p)),
                  pl.BlockSpec((1, nb, LANES), lambda b, p, i: (b, 0, p))],
        out_specs=pl.BlockSpec((1, 8, TM), lambda b, p, i: (b * N_PAIR + p, 0, i)),
        out_shape=jax.ShapeDtypeStruct((B * N_PAIR, 8, S), jnp.int32),
        compiler_params=_cparams(("parallel", "parallel", "parallel")),
        name="moba_select",
    )(q, kmean)


def _attn_body(ids_ref, q_ref, k_ref, vt_ref, slope_ref, ccol_ref, o_ref, qa_ref, m_ref, acc_ref, sa_ref, sb_ref,
               mxa_ref, mxb_ref, p_ref):
    i = pl.program_id(2)
    lane = lax.broadcasted_iota(jnp.int32, (1, LANES), 1)
    first = lane < HEAD_DIM
    q = q_ref[...]
    zero = jnp.zeros_like(q)
    qa_ref[0, :, 0:LANES] = jnp.where(first, q, zero)
    qa_ref[1, :, 0:LANES] = jnp.where(first, zero, q)
    qa_ref[:, :, LANES:] = slope_ref[0]
    ids = ids_ref[0]
    vrow = lax.broadcasted_iota(jnp.int32, (LANES, 1), 0) < HEAD_DIM

    def scores(j0, nblk):
        kj = k_ref[pl.ds(pl.multiple_of(j0 * TM, TM), nblk * TM), :]
        kaug = jnp.concatenate([kj, ccol_ref[0:nblk * TM, :]], axis=1)
        return [_dot_nt(kaug, qa_ref[h]) for h in range(2)]

    def pv(h, p, j0, nblk):
        vaug = jnp.concatenate([vt_ref[j0 + u, 0, h] for u in range(nblk)], axis=1)
        return _dot(vaug, p)

    kidx = lax.broadcasted_iota(jnp.int32, (TM, TM), 0)
    qidx = lax.broadcasted_iota(jnp.int32, (TM, TM), 1)
    for h, s in enumerate(scores(i, 1)):
        s = jnp.where(kidx <= qidx, s, -jnp.inf)
        m0 = jnp.max(s, axis=0, keepdims=True)
        m_ref[h] = m0
        acc_ref[h] = pv(h, jnp.exp2(s - m0).astype(BF16), i, 1)

    def score_block(g, u, buf, mx):
        kj = k_ref[pl.ds(pl.multiple_of((g * ATTN_GROUP + u) * TM, TM), TM), :]
        kaug = jnp.concatenate([kj, ccol_ref[u * TM:(u + 1) * TM, :]], axis=1)
        for h in range(2):
            s = _dot_nt(kaug, qa_ref[h])
            buf[h, u * TM:(u + 1) * TM, :] = s
            parts = [s[r:r + 8] for r in range(0, 64, 8)]
            for r in range(64, TM, 8):
                parts[(r // 8) % 8] = jnp.maximum(parts[(r // 8) % 8], s[r:r + 8])
            while len(parts) > 1:
                parts = [jnp.maximum(parts[n], parts[n + 1]) for n in range(0, len(parts), 2)]
            mx[h, u] = parts[0]

    def stage(g, cur, nxt, g_next):
        cur_s, cur_mx = cur
        j0 = g * ATTN_GROUP
        dist = ((j0 - i) * TM).astype(F32)
        stats = []
        for h in range(2):
            slope = slope_ref[0, h, 0:1, 0:1].astype(F32) + slope_ref[0, h, 0:1, 1:2].astype(F32)
            c0 = slope * dist
            r0 = h * MOBA_TOPK
            sels, best = [], None
            for u in range(ATTN_GROUP):
                j = j0 + u
                sel = (ids[r0:r0 + 1] == j) | (ids[r0 + 1:r0 + 2] == j) | (ids[r0 + 2:r0 + 3] == j)
                cand = jnp.where(sel, cur_mx[h, u], -jnp.inf)
                best = cand if best is None else jnp.maximum(best, cand)
                sels.append(sel)
            m_old = m_ref[h]
            m_new = jnp.maximum(m_old, jnp.max(best, axis=0, keepdims=True) + c0)
            m_ref[h] = m_new
            stats.append((jnp.exp2(m_old - m_new), [jnp.where(sel, m_new - c0, jnp.inf) for sel in sels]))
        for u in range(ATTN_GROUP):
            if nxt is not None:
                score_block(g_next, u, *nxt)
            for h in range(2):
                p_ref[h, u * TM:(u + 1) * TM, :] = jnp.exp2(cur_s[h, u * TM:(u + 1) * TM, :] - stats[h][1][u]).astype(BF16)
        for h in range(2):
            acc_ref[h] = acc_ref[h] * stats[h][0] + pv(h, p_ref[h], j0, ATTN_GROUP)

    n_group = (i + ATTN_GROUP - 1) // ATTN_GROUP
    buf_a, buf_b = (sa_ref, mxa_ref), (sb_ref, mxb_ref)

    @pl.when(n_group > 0)
    def _():
        for u in range(ATTN_GROUP):
            score_block(0, u, *buf_a)

    def pair_body(t, carry):
        stage(2 * t, buf_a, buf_b, 2 * t + 1)
        stage(2 * t + 1, buf_b, buf_a, jnp.minimum(2 * t + 2, n_group - 1))
        return carry

    lax.fori_loop(0, n_group // 2, pair_body, 0)

    @pl.when(n_group % 2 == 1)
    def _():
        stage(n_group - 1, buf_a, None, None)

    acc_a, acc_b = acc_ref[0], acc_ref[1]
    out_t = jnp.where(vrow, acc_a / acc_a[HEAD_DIM:HEAD_DIM + 1, :], acc_b / acc_b[0:1, :])
    o_ref[...] = out_t.T.astype(BF16)


def _moba_attn(ids, q, k, vt, slope_cols, ccol, B, S):
    ns = S // TM
    return pl.pallas_call(
        _attn_body,
        grid=(B, N_PAIR, ns),
        in_specs=[pl.BlockSpec((1, 8, TM), lambda b, p, i: (b * N_PAIR + p, 0, i)),
                  pl.BlockSpec((TM, LANES), lambda b, p, i: (b * ns + i, p)),
                  pl.BlockSpec((S, LANES), lambda b, p, i: (b, p)),
                  pl.BlockSpec((ns, 1, 2, LANES, TM), lambda b, p, i: (b, p, 0, 0, 0)),
                  pl.BlockSpec((1, 2, TM, LANES), lambda b, p, i: (p, 0, 0, 0)),
                  pl.BlockSpec((ATTN_GROUP * TM, LANES), lambda b, p, i: (0, 0))],
        out_specs=pl.BlockSpec((TM, LANES), lambda b, p, i: (b * ns + i, p)),
        out_shape=jax.ShapeDtypeStruct((B * S, BRANCH_WIDTH), BF16),
        scratch_shapes=[pltpu.VMEM((2, TM, 2 * LANES), BF16), pltpu.VMEM((2, 1, TM), F32),
                        pltpu.VMEM((2, LANES, TM), F32),
                        pltpu.VMEM((2, ATTN_GROUP * TM, TM), F32), pltpu.VMEM((2, ATTN_GROUP * TM, TM), F32),
                        pltpu.VMEM((2, ATTN_GROUP, 8, TM), F32), pltpu.VMEM((2, ATTN_GROUP, 8, TM), F32),
                        pltpu.VMEM((2, ATTN_GROUP * TM, TM), BF16)],
        compiler_params=_cparams(("parallel", "parallel", "arbitrary")),
        name="moba_attn",
    )(ids, q, k, vt, slope_cols, ccol)


def _pool_body(z_ref, halo_ref, w_ref, sc_ref, o_ref, ext_ref):
    i = pl.program_id(1)
    ext_ref[0:POOL_HALO, :] = jnp.where(i == 0, 0.0, halo_ref[...])
    ext_ref[POOL_HALO:, :] = z_ref[...]
    t1 = (i * TM + 1 + lax.broadcasted_iota(jnp.int32, (TM, 1), 0)).astype(F32)
    for gi, win in enumerate(POOL_WINDOWS):
        cols = slice(gi * POOL_GROUP_DIM, (gi + 1) * POOL_GROUP_DIM)
        acc = ext_ref[POOL_HALO:, cols]
        for d in range(1, win):
            acc = acc + ext_ref[POOL_HALO - d:POOL_HALO - d + TM, cols]
        p = acc / jnp.minimum(t1, float(win)) - ext_ref[POOL_HALO:, cols]
        y = _dot(p.astype(BF16), w_ref[gi])
        o_ref[:, cols] = (y * sc_ref[:, cols]).astype(BF16)


def _pool(zp, w, sc, B, S):
    ns = S // TM
    hb = TM // POOL_HALO
    return pl.pallas_call(
        _pool_body,
        grid=(B, ns),
        in_specs=[pl.BlockSpec((TM, BRANCH_WIDTH), lambda b, i: (b * ns + i, 0)),
                  pl.BlockSpec((POOL_HALO, BRANCH_WIDTH), lambda b, i: (jnp.maximum((b * ns + i) * hb - 1, 0), 0)),
                  _const_spec(w.shape), _const_spec(sc.shape)],
        out_specs=pl.BlockSpec((TM, BRANCH_WIDTH), lambda b, i: (b * ns + i, 0)),
        out_shape=jax.ShapeDtypeStruct((B * S, BRANCH_WIDTH), BF16),
        scratch_shapes=[pltpu.VMEM((POOL_HALO + TM, BRANCH_WIDTH), F32)],
        compiler_params=_cparams(("parallel", "parallel")),
        name="pool",
    )(zp, zp, w, sc)


def _seg_sum(x, seg):
    hi = x.astype(BF16)
    lo = (x - hi.astype(F32)).astype(BF16)
    outs = []
    for p in range(N_PAIR):
        cols = slice(p * LANES, (p + 1) * LANES)
        outs.append(_dot(hi[:, cols], seg) + _dot(lo[:, cols], seg))
    return jnp.concatenate(outs, axis=1)


def _rwkv_prep_body(has_vmix, z_ref, halo_ref, vf_ref, mu_ref, par_ref, wd_ref, wa_ref, wg_ref, wvm_ref,
                    rp_ref, y0_ref, g_ref, h_ref, bonus_ref, gate_ref, v_ref):
    i = pl.program_id(1)
    W = BRANCH_WIDTH
    z = z_ref[...]
    prev0 = jnp.where(i == 0, 0.0, halo_ref[7:8, :])
    row = lax.broadcasted_iota(jnp.int32, (TM, 1), 0)
    prev = jnp.where(row == 0, prev0, pltpu.roll(z, 1, axis=0))
    rw = z + (prev - z) * mu_ref[...]
    r, k, v = rw[:, 0:W], rw[:, W:2 * W], rw[:, 2 * W:3 * W]
    w_lo = rw[:, 3 * W:3 * W + LANES]
    a_lo = rw[:, 3 * W + LANES:3 * W + 2 * LANES]
    g_lo = rw[:, 3 * W + 2 * LANES:3 * W + 3 * LANES]
    vm_lo = rw[:, 3 * W + 3 * LANES:3 * W + 4 * LANES]
    w0, a0, k_k, k_a, r_k, v0 = (par_ref[n:n + 1, :] for n in range(6))
    hdot = functools.partial(jnp.dot, precision=HIGHEST, preferred_element_type=F32)

    u = w0 + hdot(jnp.tanh(w_lo), wd_ref[...])
    nu = -u
    softplus = jnp.maximum(nu, 0.0) + jnp.log(1.0 + jnp.exp(-jnp.abs(nu)))
    logw = -jnp.exp(-softplus - 0.5)
    if has_vmix:
        v = v + (vf_ref[...] - v) * _sigmoid(v0 + hdot(vm_lo, wvm_ref[...]))
    v_ref[...] = v
    a = _sigmoid(a0 + hdot(a_lo, wa_ref[...]))
    gate_ref[...] = hdot(_sigmoid(g_lo), wg_ref[...])

    lane = lax.broadcasted_iota(jnp.int32, (1, LANES), 1)
    first = lane < HEAD_DIM
    seg = ((lax.broadcasted_iota(jnp.int32, (LANES, LANES), 0) < HEAD_DIM)
           == (lax.broadcasted_iota(jnp.int32, (LANES, LANES), 1) < HEAD_DIM)).astype(BF16)
    kk = k * k_k
    kk = kk / jnp.maximum(jnp.sqrt(_seg_sum(kk * kk, seg)), 1e-12)
    k2 = k * (1.0 + (a - 1.0) * k_a)
    bonus_ref[...] = _seg_sum(r * k2 * r_k, seg) * v

    ri = lax.broadcasted_iota(jnp.int32, (TM, TM), 0)
    ci = lax.broadcasted_iota(jnp.int32, (TM, TM), 1)
    same = (ri // CHUNK) == (ci // CHUNK)
    incl = same & (ri >= ci)
    strict = same & (ri > ci)
    lmat = incl.astype(BF16)
    h1, h2, h3 = _split3(logw)
    cw = _dot(lmat, h1) + _dot(lmat, h2) + _dot(lmat, h3)
    e_neg = jnp.exp(-cw)
    at = (-kk) * jnp.exp(cw - logw)
    bt = (kk * a) * e_neg
    kt = k2 * e_neg
    rt = r * jnp.exp(cw)
    eye = (ri == ci).astype(F32)
    bd = ((lax.broadcasted_iota(jnp.int32, (LANES, LANES), 0) < HEAD_DIM)
          == (lax.broadcasted_iota(jnp.int32, (LANES, LANES), 1) < HEAD_DIM))
    eye_l = (lax.broadcasted_iota(jnp.int32, (LANES, LANES), 0)
             == lax.broadcasted_iota(jnp.int32, (LANES, LANES), 1)).astype(F32)

    for p in range(N_PAIR):
        cols = slice(p * LANES, (p + 1) * LANES)
        ap, bp, kp, rp, vp = at[:, cols], bt[:, cols], kt[:, cols], rt[:, cols], v[:, cols]
        bk = jnp.concatenate([bp, kp], axis=0).astype(BF16)
        p_pair = jnp.zeros((TM, LANES), F32)
        q_pair = jnp.zeros((TM, LANES), F32)
        rp_pair = jnp.zeros((TM, LANES), F32)
        y0_pair = jnp.zeros((TM, LANES), F32)
        for half in range(2):
            mh = first if half == 0 else jnp.logical_not(first)
            am = jnp.where(mh, ap, 0.0)
            rm = jnp.where(mh, rp, 0.0)
            vm = jnp.where(mh, vp, 0.0).astype(BF16)
            big = _dot_nt(jnp.concatenate([am, rm], axis=0).astype(BF16), bk)
            n = jnp.where(strict, big[:TM, :TM], 0.0)
            aak = jnp.where(strict, big[:TM, TM:], 0.0)
            mrb = jnp.where(incl, big[TM:, :TM], 0.0)
            mrk = jnp.where(incl, big[TM:, TM:], 0.0)
            tm = eye + n
            npow = n
            steps = CHUNK.bit_length() - 2
            for _ in range(steps):
                nb16 = npow.astype(BF16)
                npow = _dot(nb16, nb16)
                tm = tm + _dot(tm.astype(BF16), npow.astype(BF16))
            av = _dot(aak.astype(BF16), vm)
            x = _dot(tm.astype(BF16), jnp.concatenate([am, av], axis=1).astype(BF16))
            ph, qh = x[:, :LANES], x[:, LANES:]
            mrb16 = mrb.astype(BF16)
            p_pair = p_pair + ph
            q_pair = q_pair + qh
            rp_pair = rp_pair + rm + _dot(mrb16, ph.astype(BF16))
            y0_pair = y0_pair + _dot(mrb16, qh.astype(BF16)) + _dot(mrk.astype(BF16), vm)
        rp_ref[:, cols] = rp_pair
        y0_ref[:, cols] = y0_pair
        for c in range(N_CHUNK):
            rows = slice(c * CHUNK, (c + 1) * CHUNK)
            wc = jnp.exp(cw[(c + 1) * CHUNK - 1:(c + 1) * CHUNK, cols])
            bc = bp[rows].astype(BF16)
            gp = _dot_tn(p_pair[rows].astype(BF16), bc)
            hp = _dot_tn(jnp.concatenate([q_pair[rows], vp[rows]], axis=0).astype(BF16),
                         jnp.concatenate([bp[rows], kp[rows]], axis=0).astype(BF16))
            g_ref[0, c, p] = (eye_l + jnp.where(bd, gp, 0.0)) * wc
            h_ref[0, c, p] = jnp.where(bd, hp, 0.0) * wc


def _rwkv_prep(zr, vfirst, mu, par, wd, wa, wg, wvm, has_vmix, B, S):
    ns = S // TM
    T = B * S
    tile = lambda w: pl.BlockSpec((TM, w), lambda b, i: (b * ns + i, 0))
    gh_spec = pl.BlockSpec((1, N_CHUNK, N_PAIR, LANES, LANES), lambda b, i: (b * ns + i, 0, 0, 0, 0))
    gh_shape = jax.ShapeDtypeStruct((B * ns, N_CHUNK, N_PAIR, LANES, LANES), F32)
    wide = jax.ShapeDtypeStruct((T, BRANCH_WIDTH), F32)
    return pl.pallas_call(
        functools.partial(_rwkv_prep_body, has_vmix),
        grid=(B, ns),
        in_specs=[tile(RW_COLS),
                  pl.BlockSpec((8, RW_COLS), lambda b, i: (jnp.maximum((b * ns + i) * (TM // 8) - 1, 0), 0)),
                  tile(BRANCH_WIDTH), _const_spec(mu.shape), _const_spec(par.shape), _const_spec(wd.shape),
                  _const_spec(wa.shape), _const_spec(wg.shape), _const_spec(wvm.shape)],
        out_specs=[tile(BRANCH_WIDTH), tile(BRANCH_WIDTH), gh_spec, gh_spec,
                   tile(BRANCH_WIDTH), tile(BRANCH_WIDTH), tile(BRANCH_WIDTH)],
        out_shape=[wide, wide, gh_shape, gh_shape, wide, wide, wide],
        compiler_params=_cparams(("parallel", "parallel")),
        name="rwkv_prep",
    )(zr, zr, vfirst, mu, par, wd, wa, wg, wvm)


def _rwkv_scan_body(rp_ref, y0_ref, g_ref, h_ref, bonus_ref, gate_ref, ln_ref, o_ref, s_ref):
    i = pl.program_id(1)

    @pl.when(i == 0)
    def _():
        s_ref[...] = jnp.zeros_like(s_ref)

    ys = []
    for p in range(N_PAIR):
        cols = slice(p * LANES, (p + 1) * LANES)
        s = s_ref[p]
        yc = []
        for c in range(N_CHUNK):
            rows = slice(c * CHUNK, (c + 1) * CHUNK)
            s16 = s.astype(BF16)
            yc.append(_dot_nt(rp_ref[rows, cols].astype(BF16), s16) + y0_ref[rows, cols])
            s = _dot(s16, g_ref[0, c, p].astype(BF16)) + h_ref[0, c, p]
        s_ref[p] = s
        ys.append(jnp.concatenate(yc, axis=0))
    y = jnp.concatenate(ys, axis=1)
    seg = ((lax.broadcasted_iota(jnp.int32, (LANES, LANES), 0) < HEAD_DIM)
           == (lax.broadcasted_iota(jnp.int32, (LANES, LANES), 1) < HEAD_DIM)).astype(BF16)
    mu = _seg_sum(y, seg) * (1.0 / HEAD_DIM)
    d = y - mu
    var = _seg_sum(d * d, seg) * (1.0 / HEAD_DIM)
    yn = d * lax.rsqrt(var + LNX_EPS) * ln_ref[0:1, :] + ln_ref[1:2, :]
    o_ref[...] = ((yn + bonus_ref[...]) * gate_ref[...]).astype(BF16)


def _rwkv_scan(rp, y0, g, h, bonus, gate, ln, B, S):
    ns = S // TM
    tile = pl.BlockSpec((TM, BRANCH_WIDTH), lambda b, i: (b * ns + i, 0))
    gh_spec = pl.BlockSpec((1, N_CHUNK, N_PAIR, LANES, LANES), lambda b, i: (b * ns + i, 0, 0, 0, 0))
    return pl.pallas_call(
        _rwkv_scan_body,
        grid=(B, ns),
        in_specs=[tile, tile, gh_spec, gh_spec, tile, tile, _const_spec(ln.shape)],
        out_specs=tile,
        out_shape=jax.ShapeDtypeStruct((B * S, BRANCH_WIDTH), BF16),
        scratch_shapes=[pltpu.VMEM((N_PAIR, LANES, LANES), F32)],
        compiler_params=_cparams(("parallel", "arbitrary")),
        name="rwkv_scan",
    )(rp, y0, g, h, bonus, gate, ln)


def _merge_body(x_ref, gate_ref, ya_ref, yr_ref, yp_ref, wb_ref, wo_ref, g2_ref, rw_ref, rb_ref,
                x1_ref, h2_ref, eid_ref, wt_ref, cnt_ref):
    merged = jnp.zeros((TM, D_MODEL), F32)
    for bi, y_ref in enumerate((ya_ref, yr_ref, yp_ref)):
        merged = merged + gate_ref[:, bi * D_MODEL:(bi + 1) * D_MODEL] * _dot(y_ref[...], wb_ref[bi])
    x1 = x_ref[...] + _dot(merged.astype(BF16), wo_ref[...])
    x1_ref[...] = x1
    h2 = _rms(x1, g2_ref[...])
    h2_ref[...] = h2
    hi = h2.astype(BF16)
    lo = (h2 - hi.astype(F32)).astype(BF16)
    logits = _dot(hi, rw_ref[0]) + _dot(lo, rw_ref[0]) + _dot(hi, rw_ref[1]) + rb_ref[...]
    lane = lax.broadcasted_iota(jnp.int32, (TM, LANES), 1)
    ninf = -jnp.inf
    glog = jnp.where(lane < N_GROUPS, logits, ninf)
    gmax = jnp.max(glog, axis=-1, keepdims=True)
    grp = jnp.min(jnp.where(glog == gmax, lane, LANES), axis=-1, keepdims=True)
    gprob = 1.0 / jnp.sum(jnp.exp(glog - gmax), axis=-1, keepdims=True)
    e_lane = lane - N_EXPERTS
    in_grp = (e_lane >= grp * EXPERTS_PER_GROUP) & (e_lane < (grp + 1) * EXPERTS_PER_GROUP)
    elog = jnp.where(in_grp, logits, ninf)
    v1 = jnp.max(elog, axis=-1, keepdims=True)
    i1 = jnp.min(jnp.where(elog == v1, e_lane, LANES), axis=-1, keepdims=True)
    elog = jnp.where(e_lane == i1, ninf, elog)
    v2 = jnp.max(elog, axis=-1, keepdims=True)
    i2 = jnp.min(jnp.where(elog == v2, e_lane, LANES), axis=-1, keepdims=True)
    e2 = jnp.exp(v2 - v1)
    den = 1.0 + e2
    eid_ref[...] = jnp.concatenate([i1, i2], axis=1)
    wt_ref[...] = jnp.concatenate([gprob / den, gprob * e2 / den], axis=1)
    hit = ((lane == i1) | (lane == i2)).astype(F32)
    cnt_ref[0] = jnp.sum(hit, axis=0, keepdims=True).astype(jnp.int32)


def _merge(xf, gates, ya, yr, yp, wb, wo, g2, rw, rb):
    T = xf.shape[0]
    nt = T // TM
    row = lambda w: pl.BlockSpec((TM, w), lambda i: (i, 0))
    return pl.pallas_call(
        _merge_body,
        grid=(nt,),
        in_specs=[row(D_MODEL), row(GATE_COLS), row(BRANCH_WIDTH), row(BRANCH_WIDTH), row(BRANCH_WIDTH),
                  _const_spec(wb.shape), _const_spec(wo.shape), _const_spec(g2.shape), _const_spec(rw.shape),
                  _const_spec(rb.shape)],
        out_specs=[row(D_MODEL), row(D_MODEL), row(EXPERT_TOPK), row(EXPERT_TOPK),
                   pl.BlockSpec((1, 1, LANES), lambda i: (i, 0, 0))],
        out_shape=[jax.ShapeDtypeStruct((T, D_MODEL), F32), jax.ShapeDtypeStruct((T, D_MODEL), F32),
                   jax.ShapeDtypeStruct((T, EXPERT_TOPK), jnp.int32), jax.ShapeDtypeStruct((T, EXPERT_TOPK), F32),
                   jax.ShapeDtypeStruct((nt, 1, LANES), jnp.int32)],
        compiler_params=_cparams(("parallel",)),
        name="merge",
    )(xf, gates, ya, yr, yp, wb, wo, g2, rw, rb)


def _lane_excl_cumsum(x):
    lane = lax.broadcasted_iota(jnp.int32, x.shape, 1)
    inc = x
    sh = 1
    while sh < LANES:
        inc = inc + jnp.where(lane >= sh, pltpu.roll(inc, sh, axis=1), 0)
        sh *= 2
    return inc - x


def _dest_body(eid_ref, cnt_ref, dest_ref, pend_ref, base_ref):
    i = pl.program_id(0)

    @pl.when(i == 0)
    def _():
        total = jnp.sum(cnt_ref[...], axis=0)
        shift = EXPERT_ROW_BLOCK.bit_length() - 1
        padded = jnp.left_shift(jnp.right_shift(total + (EXPERT_ROW_BLOCK - 1), shift), shift)
        pstart = _lane_excl_cumsum(padded)
        base_ref[...] = pstart
        pend_ref[...] = pstart + padded

    lane = lax.broadcasted_iota(jnp.int32, (TM, LANES), 1)
    e0, e1 = eid_ref[:, 0:1], eid_ref[:, 1:2]
    oh0, oh1 = lane == e0, lane == e1
    both = (oh0 | oh1).astype(BF16)
    lower = (lax.broadcasted_iota(jnp.int32, (TM, TM), 0)
             > lax.broadcasted_iota(jnp.int32, (TM, TM), 1)).astype(BF16)
    pos = _dot(lower, both) + base_ref[...].astype(F32)
    d0 = jnp.sum(jnp.where(oh0, pos, 0.0), axis=-1, keepdims=True)
    d1 = jnp.sum(jnp.where(oh1, pos, 0.0), axis=-1, keepdims=True)
    dest_ref[...] = jnp.concatenate([d0, d1], axis=1).astype(jnp.int32)
    base_ref[...] = base_ref[...] + jnp.sum(both.astype(F32), axis=0, keepdims=True).astype(jnp.int32)


def _moe_dest(eid, cnt):
    T = eid.shape[0]
    nt = T // TM
    return pl.pallas_call(
        _dest_body,
        grid=(nt,),
        in_specs=[pl.BlockSpec((TM, EXPERT_TOPK), lambda i: (i, 0)),
                  pl.BlockSpec((nt, 1, LANES), lambda i: (0, 0, 0))],
        out_specs=[pl.BlockSpec((TM, EXPERT_TOPK), lambda i: (i, 0)),
                   pl.BlockSpec((1, LANES), lambda i: (0, 0))],
        out_shape=[jax.ShapeDtypeStruct((T, EXPERT_TOPK), jnp.int32), jax.ShapeDtypeStruct((1, LANES), jnp.int32)],
        scratch_shapes=[pltpu.VMEM((1, LANES), jnp.int32)],
        compiler_params=_cparams(("arbitrary",)),
        name="moe_dest",
    )(eid, cnt)


def _scatter_body(dest_ref, h_ref, buf_in_ref, buf_ref, sem):
    del buf_in_ref
    i = pl.program_id(0)

    def issue(t, carry):
        for s in range(EXPERT_TOPK):
            d = dest_ref[(i * TM + t) * EXPERT_TOPK + s]
            pltpu.make_async_copy(h_ref.at[pl.ds(t, 1), :], buf_ref.at[pl.ds(d, 1), :], sem).start()
        return carry

    lax.fori_loop(0, TM, issue, 0)
    for _ in range(EXPERT_TOPK):
        pltpu.make_async_copy(h_ref, buf_ref.at[pl.ds(0, TM), :], sem).wait()


def _moe_scatter(dest_flat, h2, buf0):
    T = h2.shape[0]
    nt = T // TM
    return pl.pallas_call(
        _scatter_body,
        grid_spec=pltpu.PrefetchScalarGridSpec(
            num_scalar_prefetch=1, grid=(nt,),
            in_specs=[pl.BlockSpec((TM, D_MODEL), lambda i, d: (i, 0)), pl.BlockSpec(memory_space=pl.ANY)],
            out_specs=pl.BlockSpec(memory_space=pl.ANY),
            scratch_shapes=[pltpu.SemaphoreType.DMA(())]),
        out_shape=jax.ShapeDtypeStruct(buf0.shape, F32),
        input_output_aliases={2: 0},
        compiler_params=_cparams(("arbitrary",)),
        name="moe_scatter",
    )(dest_flat, h2, buf0)


def _expert_body(be_ref, x_ref, w1_ref, w3_ref, w2_ref, o_ref):
    del be_ref
    xb = x_ref[...].astype(BF16)
    h1 = _dot(xb, w1_ref[0])
    h3 = _dot(xb, w3_ref[0])
    hid = (h1 * _sigmoid(h1)) * h3
    o_ref[...] = _dot(hid.astype(BF16), w2_ref[0])


def _moe_experts(blk_e, buf, w1, w3, w2):
    R = buf.shape[0]
    nblk = R // EXPERT_ROW_BLOCK
    return pl.pallas_call(
        _expert_body,
        grid_spec=pltpu.PrefetchScalarGridSpec(
            num_scalar_prefetch=1, grid=(nblk,),
            in_specs=[pl.BlockSpec((EXPERT_ROW_BLOCK, D_MODEL), lambda b, e: (b, 0)),
                      pl.BlockSpec((1, D_MODEL, EXPERT_HIDDEN), lambda b, e: (e[b], 0, 0)),
                      pl.BlockSpec((1, D_MODEL, EXPERT_HIDDEN), lambda b, e: (e[b], 0, 0)),
                      pl.BlockSpec((1, EXPERT_HIDDEN, D_MODEL), lambda b, e: (e[b], 0, 0))],
            out_specs=pl.BlockSpec((EXPERT_ROW_BLOCK, D_MODEL), lambda b, e: (b, 0))),
        out_shape=jax.ShapeDtypeStruct((R, D_MODEL), F32),
        compiler_params=_cparams(("arbitrary",)),
        name="moe_experts",
    )(blk_e, buf, w1, w3, w2)


def _combine_body(final, dest_ref, x_ref, wt_ref, gf_ref, yb_ref, o_ref, g0_ref, g1_ref, sem):
    i = pl.program_id(0)
    bufs = (g0_ref, g1_ref)

    def issue(t, carry):
        for s in range(EXPERT_TOPK):
            d = dest_ref[(i * TM + t) * EXPERT_TOPK + s]
            pltpu.make_async_copy(yb_ref.at[pl.ds(d, 1), :], bufs[s].at[pl.ds(t, 1), :], sem).start()
        return carry

    lax.fori_loop(0, TM, issue, 0)
    for s in range(EXPERT_TOPK):
        pltpu.make_async_copy(yb_ref.at[pl.ds(0, TM), :], bufs[s], sem).wait()
    wt = wt_ref[...]
    x = x_ref[...] + (g0_ref[...] * wt[:, 0:1] + g1_ref[...] * wt[:, 1:2])
    if final:
        x = _rms(x, gf_ref[...])
    o_ref[...] = x


def _moe_combine(dest_flat, x1, wts, gf, yb, final):
    T = x1.shape[0]
    nt = T // TM
    return pl.pallas_call(
        functools.partial(_combine_body, final),
        grid_spec=pltpu.PrefetchScalarGridSpec(
            num_scalar_prefetch=1, grid=(nt,),
            in_specs=[pl.BlockSpec((TM, D_MODEL), lambda i, d: (i, 0)),
                      pl.BlockSpec((TM, EXPERT_TOPK), lambda i, d: (i, 0)),
                      pl.BlockSpec((1, D_MODEL), lambda i, d: (0, 0)),
                      pl.BlockSpec(memory_space=pl.ANY)],
            out_specs=pl.BlockSpec((TM, D_MODEL), lambda i, d: (i, 0)),
            scratch_shapes=[pltpu.VMEM((TM, D_MODEL), F32), pltpu.VMEM((TM, D_MODEL), F32),
                            pltpu.SemaphoreType.DMA(())]),
        out_shape=jax.ShapeDtypeStruct((T, D_MODEL), F32),
        compiler_params=_cparams(("arbitrary",)),
        name="moe_combine",
    )(dest_flat, x1, wts, gf, yb)


def _pad_cols(w, n):
    return jnp.pad(w, ((0, 0), (0, n - w.shape[1])))


def _rwkv_cols(w_rw, w_vm):
    W = BRANCH_WIDTH
    o1, o2, o3, o4 = 3 * W, 3 * W + DECAY_LORA, 3 * W + DECAY_LORA + ICLR_LORA, 3 * W + DECAY_LORA + ICLR_LORA + GATE_LORA
    return jnp.concatenate([w_rw[:, :o1], _pad_cols(w_rw[:, o1:o2], LANES), _pad_cols(w_rw[:, o2:o3], LANES),
                            w_rw[:, o3:o4], _pad_cols(w_vm, LANES)], axis=1)


def _pad_rows(w, n):
    return jnp.pad(w, ((0, n - w.shape[0]), (0, 0)))


def kernel(x, norm1_g, w_in, w_vmix, rwkv_mu, vmix_mu, w0, w_decay_up, a0, w_iclr_up, w_gate_up, k_k, k_a, r_k,
           lnx_g, lnx_b, v0, w_vmix_up, pool_w, pool_scale, w_branch, w_out, norm2_g, router_grp_w, router_grp_b,
           router_exp_w, router_exp_b, exp_w1, exp_w3, exp_w2, final_norm_g):
    B, S, D = x.shape
    assert D == D_MODEL and S % (TM * ATTN_GROUP) == 0
    T = B * S
    ns = S // TM
    nb = ns
    xf = x.reshape(T, D)

    slopes = jnp.asarray([2.0 ** (-8.0 * (h + 1) / ATTN_HEADS) * LOG2E for h in range(ATTN_HEADS)], F32)
    slopes = slopes.reshape(N_PAIR, 2, 1)
    s_hi = slopes.astype(BF16).astype(F32)
    s_lo = slopes - s_hi
    slope_cols = jnp.zeros((N_PAIR, 2, TM, LANES), F32)
    for col, val in enumerate((s_hi, s_lo, s_hi * TM, s_lo * TM)):
        slope_cols = slope_cols.at[:, :, :, col].set(val)
    slope_cols = slope_cols.astype(BF16)
    key_row = jnp.arange(ATTN_GROUP * TM, dtype=jnp.int32)
    in_blk, blk_no = (key_row % TM).astype(F32), (key_row // TM).astype(F32)
    ccol = jnp.zeros((ATTN_GROUP * TM, LANES), F32)
    for col, val in enumerate((in_blk, in_blk, blk_no, blk_no)):
        ccol = ccol.at[:, col].set(val)
    ccol = ccol.astype(BF16)

    n_assign = T * EXPERT_TOPK
    R = -(-(n_assign + N_EXPERTS * (EXPERT_ROW_BLOCK - 1)) // EXPERT_ROW_BLOCK) * EXPERT_ROW_BLOCK
    nblk = R // EXPERT_ROW_BLOCK

    v_first = jnp.zeros((T, BRANCH_WIDTH), F32)
    for l in range(DEPTH):
        wl = w_in[l]
        wa = wl[:, :2 * BRANCH_WIDTH].astype(BF16)
        wvt = wl[:, 2 * BRANCH_WIDTH:ATTN_COLS].T.astype(BF16)
        wg =wl[:, GATE_OFF:GATE_OFF + GATE_COLS].astype(BF16)
        wp = wl[:, POOL_OFF:POOL_OFF + BRANCH_WIDTH].astype(BF16)
        has_vmix = l > 0
        w_vm = w_vmix[l - 1] if has_vmix else jnp.zeros((D, VMIX_LORA), F32)
        mu_vm = vmix_mu[l - 1] if has_vmix else jnp.zeros((VMIX_LORA,), F32)
        wr = _rwkv_cols(wl[:, RWKV_OFF:], w_vm).astype(BF16)
        mu = _rwkv_cols(rwkv_mu[l][None, :], mu_vm[None, :])
        q, k, vt, gates, zp, zr, kmean = _inproj(xf, norm1_g[l][None, :], wa, wvt, wg, wp, wr)

        ids = _moba_select(q, kmean.reshape(B, nb, BRANCH_WIDTH), B, S)
        y_attn = _moba_attn(ids, q, k, vt, slope_cols, ccol, B, S)

        par = jnp.stack([w0[l], a0[l], k_k[l], k_a[l], r_k[l].reshape(-1),
                         v0[l - 1] if has_vmix else jnp.zeros((BRANCH_WIDTH,), F32),
                         jnp.zeros((BRANCH_WIDTH,), F32), jnp.zeros((BRANCH_WIDTH,), F32)])
        wvm_up = _pad_rows(w_vmix_up[l - 1], LANES) if has_vmix else jnp.zeros((LANES, BRANCH_WIDTH), F32)
        rp, y0, gm, hm, bonus, rgate, v_cur = _rwkv_prep(
            zr, v_first, mu, par, _pad_rows(w_decay_up[l], LANES), _pad_rows(w_iclr_up[l], LANES), w_gate_up[l],
            wvm_up, has_vmix, B, S)
        if l == 0:
            v_first = v_cur
        y_rwkv = _rwkv_scan(rp, y0, gm, hm, bonus, rgate, jnp.stack([lnx_g[l], lnx_b[l]]), B, S)

        y_pool = _pool(zp, pool_w[l].astype(BF16), pool_scale[l][None, :], B, S)

        rw = jnp.zeros((D, LANES), F32).at[:, :N_GROUPS].set(router_grp_w[l]).at[:, N_EXPERTS:2 * N_EXPERTS].set(router_exp_w[l])
        rw_hi = rw.astype(BF16)
        rw_lo = (rw - rw_hi.astype(F32)).astype(BF16)
        rb = jnp.zeros((1, LANES), F32).at[0, :N_GROUPS].set(router_grp_b[l]).at[0, N_EXPERTS:2 * N_EXPERTS].set(router_exp_b[l])
        x1, h2, eid, wts, cnt = _merge(xf, gates, y_attn, y_rwkv, y_pool, w_branch[l].astype(BF16),
                                       w_out[l].astype(BF16), norm2_g[l][None, :], jnp.stack([rw_hi, rw_lo]), rb)

        dest, pends = _moe_dest(eid, cnt)
        dest_flat = dest.reshape(-1)
        blk_row = jnp.arange(nblk, dtype=jnp.int32)[:, None] * EXPERT_ROW_BLOCK
        blk_e = jnp.minimum(jnp.sum((pends[0, :N_EXPERTS][None, :] <= blk_row).astype(jnp.int32), axis=1),
                            N_EXPERTS - 1)
        buf = _moe_scatter(dest_flat, h2, jnp.zeros((R, D), F32))
        yb = _moe_experts(blk_e, buf, exp_w1[l].astype(BF16), exp_w3[l].astype(BF16), exp_w2[l].astype(BF16))
        xf = _moe_combine(dest_flat, x1, wts, final_norm_g[None, :], yb, l == DEPTH - 1)
    return xf.reshape(B, S, D)
```

```python
import functools

import jax
import jax.numpy as jnp
from jax import lax
from jax.experimental import pallas as pl
from jax.experimental.pallas import tpu as pltpu

F32 = jnp.float32
BF16 = jnp.bfloat16
HIGHEST = lax.Precision.HIGHEST
LOG2E = 1.4426950408889634

D_MODEL = 1024
DEPTH = 2
BRANCH_WIDTH = 512
N_BRANCH = 3
ATTN_HEADS = 8
HEAD_DIM = 64
MOBA_BLOCK = 256
MOBA_TOPK = 3
DECAY_LORA = 64
ICLR_LORA = 64
GATE_LORA = 128
VMIX_LORA = 32
LNX_EPS = 64e-5
POOL_WINDOWS = (2, 4, 8, 16)
POOL_GROUP_DIM = 128
N_GROUPS = 4
EXPERTS_PER_GROUP = 8
N_EXPERTS = 32
EXPERT_TOPK = 2
EXPERT_HIDDEN = 512
RMS_EPS = 1e-6
ATTN_COLS = 3 * BRANCH_WIDTH
GATE_OFF = ATTN_COLS
GATE_COLS = N_BRANCH * D_MODEL
POOL_OFF = GATE_OFF + GATE_COLS
RWKV_OFF = POOL_OFF + BRANCH_WIDTH

LANES = 128
TM = 256
CHUNK = 64
N_CHUNK = TM // CHUNK
N_PAIR = ATTN_HEADS // 2
RW_COLS = 2048
POOL_HALO = 16
ATTN_GROUP = 4
MOE_BLOCK = 256
DMA_ISSUE_UNROLL = 8
V_ROWS = 80
VMEM_LIMIT = 56 * 1024 * 1024


def _cparams(sem):
    return pltpu.CompilerParams(dimension_semantics=sem, vmem_limit_bytes=VMEM_LIMIT)


def _const_spec(shape):
    nd = len(shape)
    return pl.BlockSpec(shape, lambda *a: (0,) * nd, pipeline_mode=pl.Buffered(1))


def _sigmoid(x):
    return 1.0 / (1.0 + jnp.exp(-x))


def _rms(x, g):
    return x * lax.rsqrt(jnp.mean(x * x, axis=-1, keepdims=True) + RMS_EPS) * g


def _split3(x):
    h1 = x.astype(BF16)
    r1 = x - h1.astype(F32)
    h2 = r1.astype(BF16)
    h3 = (r1 - h2.astype(F32)).astype(BF16)
    return h1, h2, h3


def _dot(a, b):
    return jnp.dot(a, b, preferred_element_type=F32)


def _dot_nt(a, b):
    return lax.dot_general(a, b, (((1,), (1,)), ((), ())), preferred_element_type=F32)


def _dot_tn(a, b):
    return lax.dot_general(a, b, (((0,), (0,)), ((), ())), preferred_element_type=F32)


def _inproj_body(x_ref, g_ref, wa_ref, wvt_ref, wg_ref, wp_ref, wr_ref,
                 q_ref, k_ref, vt_ref, gate_ref, zp_ref, zr_ref, km_ref):
    hb = _rms(x_ref[...], g_ref[...]).astype(BF16)
    za = _dot(hb, wa_ref[...])
    q_ref[...] = (za[:, :BRANCH_WIDTH] * (HEAD_DIM ** -0.5 * LOG2E)).astype(BF16)
    k = za[:, BRANCH_WIDTH:]
    k_ref[...] = k.astype(BF16)
    vt = _dot_nt(wvt_ref[...], hb)
    ones = jnp.ones((V_ROWS - HEAD_DIM, TM), BF16)
    for hd in range(ATTN_HEADS):
        vt_ref[0, hd // 2, hd % 2, 0:HEAD_DIM, :] = vt[hd * HEAD_DIM:(hd + 1) * HEAD_DIM].astype(BF16)
        vt_ref[0, hd // 2, hd % 2, HEAD_DIM:, :] = ones
    km_ref[0] = jnp.mean(k, axis=0, keepdims=True)
    gate_ref[...] = _sigmoid(_dot(hb, wg_ref[...]))
    zp_ref[...] = _dot(hb, wp_ref[...])
    zr_ref[...] = _dot(hb, wr_ref[...])


def _inproj(xf, g, wa, wvt, wg, wp, wr):
    T = xf.shape[0]
    nt = T // TM
    row = lambda w: pl.BlockSpec((TM, w), lambda i: (i, 0))
    return pl.pallas_call(
        _inproj_body,
        grid=(nt,),
        in_specs=[row(D_MODEL), _const_spec((1, D_MODEL)), _const_spec(wa.shape), _const_spec(wvt.shape),
                  _const_spec(wg.shape), _const_spec(wp.shape), _const_spec(wr.shape)],
        out_specs=[row(BRANCH_WIDTH), row(BRANCH_WIDTH),
                   pl.BlockSpec((1, N_PAIR, 2, V_ROWS, TM), lambda i: (i, 0, 0, 0, 0)),
                   row(GATE_COLS), row(BRANCH_WIDTH), row(RW_COLS),
                   pl.BlockSpec((1, 1, BRANCH_WIDTH), lambda i: (i, 0, 0))],
        out_shape=[jax.ShapeDtypeStruct((T, BRANCH_WIDTH), BF16), jax.ShapeDtypeStruct((T, BRANCH_WIDTH), BF16),
                   jax.ShapeDtypeStruct((nt, N_PAIR, 2, V_ROWS, TM), BF16),
                   jax.ShapeDtypeStruct((T, GATE_COLS), F32), jax.ShapeDtypeStruct((T, BRANCH_WIDTH), F32),
                   jax.ShapeDtypeStruct((T, RW_COLS), F32), jax.ShapeDtypeStruct((nt, 1, BRANCH_WIDTH), F32)],
        compiler_params=_cparams(("parallel",)),
        name="inproj",
    )(xf, g, wa, wvt, wg, wp, wr)


def _select_body(q_ref, km_ref, o_ref):
    i = pl.program_id(2)
    q = q_ref[...]
    km = km_ref[0]
    nb = km.shape[0]
    lane = lax.broadcasted_iota(jnp.int32, (1, LANES), 1)
    blk = lax.broadcasted_iota(jnp.int32, (nb, TM), 0)
    rows = []
    for half in range(2):
        in_head = (lane < HEAD_DIM) if half == 0 else (lane >= HEAD_DIM)
        g = sum(_dot_nt(part, q) for part in _split3(jnp.where(in_head, km, 0.0)))
        g = jnp.where(blk < i, g, -jnp.inf)
        for _ in range(MOBA_TOPK):
            mx = jnp.max(g, axis=0, keepdims=True)
            idx = jnp.min(jnp.where(g == mx, blk, nb), axis=0, keepdims=True)
            rows.append(jnp.where(mx > -jnp.inf, idx, -1))
            g = jnp.where(blk == idx, -jnp.inf, g)
    rows += [jnp.full((1, TM), -1, jnp.int32)] * 2
    o_ref[0] = jnp.concatenate(rows, axis=0)


def _moba_select(q, kmean, B, S):
    ns = S // TM
    nb = kmean.shape[1]
    return pl.pallas_call(
        _select_body,
        grid=(B, N_PAIR, ns),
        in_specs=[pl.BlockSpec((TM, LANES), lambda b, p, i: (b * ns + i, p)),
                  pl.BlockSpec((1, nb, LANES), lambda b, p, i: (b, 0, p))],
        out_specs=pl.BlockSpec((1, 8, TM), lambda b, p, i: (b * N_PAIR + p, 0, i)),
        out_shape=jax.ShapeDtypeStruct((B * N_PAIR, 8, S), jnp.int32),
        compiler_params=_cparams(("parallel", "parallel", "parallel")),
        name="moba_select",
    )(q, kmean)


def _attn_body(ids_ref, q_ref, k_ref, vt_ref, slope_ref, ccol_ref, o_ref, qa_ref, m_ref, acc_ref, sa_ref, sb_ref,
               mxa_ref, mxb_ref, p_ref):
    i = pl.program_id(2)
    lane = lax.broadcasted_iota(jnp.int32, (1, LANES), 1)
    first = lane < HEAD_DIM
    q = q_ref[...]
    zero = jnp.zeros_like(q)
    qa_ref[0, :, 0:LANES] = jnp.where(first, q, zero)
    qa_ref[1, :, 0:LANES] = jnp.where(first, zero, q)
    qa_ref[:, :, LANES:] = slope_ref[0]
    ids = ids_ref[0]

    def scores(j0, nblk):
        kj = k_ref[pl.ds(pl.multiple_of(j0 * TM, TM), nblk * TM), :]
        kaug = jnp.concatenate([kj, ccol_ref[0:nblk * TM, :]], axis=1)
        return [_dot_nt(kaug, qa_ref[h]) for h in range(2)]

    def pv(h, p, j0, nblk):
        vaug = jnp.concatenate([vt_ref[j0 + u, 0, h] for u in range(nblk)], axis=1)
        return _dot(vaug, p)

    kidx = lax.broadcasted_iota(jnp.int32, (TM, TM), 0)
    qidx = lax.broadcasted_iota(jnp.int32, (TM, TM), 1)
    def own_block():
        for h, s in enumerate(scores(i, 1)):
            s = jnp.where(kidx <= qidx, s, -jnp.inf)
            m0 = jnp.max(s, axis=0, keepdims=True)
            m_ref[h] = m0
            acc_ref[h] = pv(h, jnp.exp2(s - m0).astype(BF16), i, 1)

    def score_block(g, u, buf, mx):
        kj = k_ref[pl.ds(pl.multiple_of((g * ATTN_GROUP + u) * TM, TM), TM), :]
        kaug = jnp.concatenate([kj, ccol_ref[u * TM:(u + 1) * TM, :]], axis=1)
        for h in range(2):
            s = _dot_nt(kaug, qa_ref[h])
            buf[h, u * TM:(u + 1) * TM, :] = s
            parts = [s[r:r + 8] for r in range(0, 64, 8)]
            for r in range(64, TM, 8):
                parts[(r // 8) % 8] = jnp.maximum(parts[(r // 8) % 8], s[r:r + 8])
            while len(parts) > 1:
                parts = [jnp.maximum(parts[n], parts[n + 1]) for n in range(0, len(parts), 2)]
            mx[h, u] = parts[0]

    def stage(g, cur, nxt, g_next):
        cur_s, cur_mx = cur
        j0 = g * ATTN_GROUP
        dist = ((j0 - i) * TM).astype(F32)
        stats = []
        for h in range(2):
            slope = slope_ref[0, h, 0:1, 0:1].astype(F32) + slope_ref[0, h, 0:1, 1:2].astype(F32)
            c0 = slope * dist
            r0 = h * MOBA_TOPK
            sels, best = [], None
            for u in range(ATTN_GROUP):
                j = j0 + u
                sel = (ids[r0:r0 + 1] == j) | (ids[r0 + 1:r0 + 2] == j) | (ids[r0 + 2:r0 + 3] == j)
                cand = jnp.where(sel, cur_mx[h, u], -jnp.inf)
                best = cand if best is None else jnp.maximum(best, cand)
                sels.append(sel)
            m_old = m_ref[h]
            m_new = jnp.maximum(m_old, jnp.max(best, axis=0, keepdims=True) + c0)
            m_ref[h] = m_new
            stats.append((jnp.exp2(m_old - m_new), [jnp.where(sel, m_new - c0, jnp.inf) for sel in sels]))
        for u in range(ATTN_GROUP):
            if nxt is not None:
                score_block(g_next, u, *nxt)
            for h in range(2):
                p_ref[h, u * TM:(u + 1) * TM, :] = jnp.exp2(cur_s[h, u * TM:(u + 1) * TM, :] - stats[h][1][u]).astype(BF16)
        for h in range(2):
            acc_ref[h] = acc_ref[h] * stats[h][0] + pv(h, p_ref[h], j0, ATTN_GROUP)

    n_group = (i + ATTN_GROUP - 1) // ATTN_GROUP
    buf_a, buf_b = (sa_ref, mxa_ref), (sb_ref, mxb_ref)

    for u in range(ATTN_GROUP):
        score_block(0, u, *buf_a)
    own_block()

    def pair_body(t, carry):
        stage(2 * t, buf_a, buf_b, 2 * t + 1)
        stage(2 * t + 1, buf_b, buf_a, jnp.minimum(2 * t + 2, n_group - 1))
        return carry

    lax.fori_loop(0, n_group // 2, pair_body, 0)

    @pl.when(n_group % 2 == 1)
    def _():
        stage(n_group - 1, buf_a, None, None)

    acc_a, acc_b = acc_ref[0], acc_ref[1]
    out_t = jnp.concatenate([acc[:HEAD_DIM] / acc[HEAD_DIM:HEAD_DIM + 1] for acc in (acc_a, acc_b)], axis=0)
    o_ref[...] = out_t.T.astype(BF16)


def _moba_attn(ids, q, k, vt, slope_cols, ccol, B, S):
    ns = S // TM
    return pl.pallas_call(
        _attn_body,
        grid=(B, N_PAIR, ns),
        in_specs=[pl.BlockSpec((1, 8, TM), lambda b, p, i: (b * N_PAIR + p, 0, i)),
                  pl.BlockSpec((TM, LANES), lambda b, p, i: (b * ns + i, p)),
                  pl.BlockSpec((S, LANES), lambda b, p, i: (b, p)),
                  pl.BlockSpec((ns, 1, 2, V_ROWS, TM), lambda b, p, i: (b, p, 0, 0, 0)),
                  pl.BlockSpec((1, 2, TM, LANES), lambda b, p, i: (p, 0, 0, 0)),
                  pl.BlockSpec((ATTN_GROUP * TM, LANES), lambda b, p, i: (0, 0))],
        out_specs=pl.BlockSpec((TM, LANES), lambda b, p, i: (b * ns + i, p)),
        out_shape=jax.ShapeDtypeStruct((B * S, BRANCH_WIDTH), BF16),
        scratch_shapes=[pltpu.VMEM((2, TM, 2 * LANES), BF16), pltpu.VMEM((2, 1, TM), F32),
                        pltpu.VMEM((2, V_ROWS, TM), F32),
                        pltpu.VMEM((2, ATTN_GROUP * TM, TM), F32), pltpu.VMEM((2, ATTN_GROUP * TM, TM), F32),
                        pltpu.VMEM((2, ATTN_GROUP, 8, TM), F32), pltpu.VMEM((2, ATTN_GROUP, 8, TM), F32),
                        pltpu.VMEM((2, ATTN_GROUP * TM, TM), BF16)],
        compiler_params=_cparams(("parallel", "parallel", "arbitrary")),
        name="moba_attn",
    )(ids, q, k, vt, slope_cols, ccol)


def _pool_body(z_ref, halo_ref, w_ref, sc_ref, o_ref, ext_ref):
    i = pl.program_id(1)
    ext_ref[0:POOL_HALO, :] = jnp.where(i == 0, 0.0, halo_ref[...])
    ext_ref[POOL_HALO:, :] = z_ref[...]
    t1 = (i * TM + 1 + lax.broadcasted_iota(jnp.int32, (TM, 1), 0)).astype(F32)
    for gi, win in enumerate(POOL_WINDOWS):
        cols = slice(gi * POOL_GROUP_DIM, (gi + 1) * POOL_GROUP_DIM)
        acc = ext_ref[POOL_HALO:, cols]
        for d in range(1, win):
            acc = acc + ext_ref[POOL_HALO - d:POOL_HALO - d + TM, cols]
        p = acc / jnp.minimum(t1, float(win)) - ext_ref[POOL_HALO:, cols]
        y = _dot(p.astype(BF16), w_ref[gi])
        o_ref[:, cols] = (y * sc_ref[:, cols]).astype(BF16)


def _pool(zp, w, sc, B, S):
    ns = S // TM
    hb = TM // POOL_HALO
    return pl.pallas_call(
        _pool_body,
        grid=(B, ns),
        in_specs=[pl.BlockSpec((TM, BRANCH_WIDTH), lambda b, i: (b * ns + i, 0)),
                  pl.BlockSpec((POOL_HALO, BRANCH_WIDTH), lambda b, i: (jnp.maximum((b * ns + i) * hb - 1, 0), 0)),
                  _const_spec(w.shape), _const_spec(sc.shape)],
        out_specs=pl.BlockSpec((TM, BRANCH_WIDTH), lambda b, i: (b * ns + i, 0)),
        out_shape=jax.ShapeDtypeStruct((B * S, BRANCH_WIDTH), BF16),
        scratch_shapes=[pltpu.VMEM((POOL_HALO + TM, BRANCH_WIDTH), F32)],
        compiler_params=_cparams(("parallel", "parallel")),
        name="pool",
    )(zp, zp, w, sc)


def _seg_sum(x, seg):
    hi = x.astype(BF16)
    lo = (x - hi.astype(F32)).astype(BF16)
    outs = []
    for p in range(N_PAIR):
        cols = slice(p * LANES, (p + 1) * LANES)
        outs.append(_dot(hi[:, cols], seg) + _dot(lo[:, cols], seg))
    return jnp.concatenate(outs, axis=1)


def _rwkv_prep_body(has_vmix, z_ref, halo_ref, vf_ref, mu_ref, par_ref, wd_ref, wa_ref, wg_ref, wvm_ref,
                    rp_ref, y0_ref, g_ref, h_ref, bonus_ref, gate_ref, v_ref):
    i = pl.program_id(1)
    W = BRANCH_WIDTH
    z = z_ref[...]
    prev0 = jnp.where(i == 0, 0.0, halo_ref[7:8, :])
    row = lax.broadcasted_iota(jnp.int32, (TM, 1), 0)
    prev = jnp.where(row == 0, prev0, pltpu.roll(z, 1, axis=0))
    rw = z + (prev - z) * mu_ref[...]
    r, k, v = rw[:, 0:W], rw[:, W:2 * W], rw[:, 2 * W:3 * W]
    w_lo = rw[:, 3 * W:3 * W + LANES]
    a_lo = rw[:, 3 * W + LANES:3 * W + 2 * LANES]
    g_lo = rw[:, 3 * W + 2 * LANES:3 * W + 3 * LANES]
    vm_lo = rw[:, 3 * W + 3 * LANES:3 * W + 4 * LANES]
    w0, a0, k_k, k_a, r_k, v0 = (par_ref[n:n + 1, :] for n in range(6))
    hdot = functools.partial(jnp.dot, precision=HIGHEST, preferred_element_type=F32)

    u = w0 + hdot(jnp.tanh(w_lo), wd_ref[...])
    nu = -u
    softplus = jnp.maximum(nu, 0.0) + jnp.log(1.0 + jnp.exp(-jnp.abs(nu)))
    logw = -jnp.exp(-softplus - 0.5)
    if has_vmix:
        v = v + (vf_ref[...] - v) * _sigmoid(v0 + hdot(vm_lo, wvm_ref[...]))
    v_ref[...] = v
    a = _sigmoid(a0 + hdot(a_lo, wa_ref[...]))
    gate_ref[...] = hdot(_sigmoid(g_lo), wg_ref[...])

    lane = lax.broadcasted_iota(jnp.int32, (1, LANES), 1)
    first = lane < HEAD_DIM
    seg = ((lax.broadcasted_iota(jnp.int32, (LANES, LANES), 0) < HEAD_DIM)
           == (lax.broadcasted_iota(jnp.int32, (LANES, LANES), 1) < HEAD_DIM)).astype(BF16)
    kk = k * k_k
    kk = kk / jnp.maximum(jnp.sqrt(_seg_sum(kk * kk, seg)), 1e-12)
    k2 = k * (1.0 + (a - 1.0) * k_a)
    bonus_ref[...] = _seg_sum(r * k2 * r_k, seg) * v

    ri = lax.broadcasted_iota(jnp.int32, (TM, TM), 0)
    ci = lax.broadcasted_iota(jnp.int32, (TM, TM), 1)
    same = (ri // CHUNK) == (ci // CHUNK)
    incl = same & (ri >= ci)
    strict = same & (ri > ci)
    lmat = incl.astype(BF16)
    h1, h2, h3 = _split3(logw)
    cw = _dot(lmat, h1) + _dot(lmat, h2) + _dot(lmat, h3)
    e_neg = jnp.exp(-cw)
    at = (-kk) * jnp.exp(cw - logw)
    bt = (kk * a) * e_neg
    kt = k2 * e_neg
    rt = r * jnp.exp(cw)
    eye_pk = (lax.broadcasted_iota(jnp.int32, (CHUNK, TM), 0)
              == lax.broadcasted_iota(jnp.int32, (CHUNK, TM), 1) % CHUNK).astype(F32)
    bd = ((lax.broadcasted_iota(jnp.int32, (LANES, LANES), 0) < HEAD_DIM)
          == (lax.broadcasted_iota(jnp.int32, (LANES, LANES), 1) < HEAD_DIM))
    eye_l = (lax.broadcasted_iota(jnp.int32, (LANES, LANES), 0)
             == lax.broadcasted_iota(jnp.int32, (LANES, LANES), 1)).astype(F32)

    def block_diag(packed):
        return jnp.where(same, jnp.concatenate([packed] * N_CHUNK, axis=0), jnp.zeros((), packed.dtype))

    heads = []
    for hd in range(ATTN_HEADS):
        cols = slice((hd // 2) * LANES, (hd // 2 + 1) * LANES)
        mh = first if hd % 2 == 0 else jnp.logical_not(first)
        am = jnp.where(mh, at[:, cols], 0.0)
        rm = jnp.where(mh, rt[:, cols], 0.0)
        vm = jnp.where(mh, v[:, cols], 0.0).astype(BF16)
        bk = jnp.concatenate([bt[:, cols], kt[:, cols]], axis=0).astype(BF16)
        big = _dot_nt(jnp.concatenate([am, rm], axis=0).astype(BF16), bk)
        n = jnp.where(strict, big[:TM, :TM], 0.0)
        n_pk = n[0:CHUNK]
        for c in range(1, N_CHUNK):
            n_pk = n_pk + n[c * CHUNK:(c + 1) * CHUNK]
        heads.append(dict(
            am=am, rm=rm, vm=vm, t_pk=eye_pk + n_pk, p_pk=n_pk, p_bd=n.astype(BF16),
            aak=jnp.where(strict, big[:TM, TM:], 0.0).astype(BF16),
            mrb=jnp.where(incl, big[TM:, :TM], 0.0).astype(BF16),
            mrk=jnp.where(incl, big[TM:, TM:], 0.0).astype(BF16)))
    for _ in range(CHUNK.bit_length() - 2):
        for hs in heads:
            hs["p_pk"] = _dot(hs["p_pk"].astype(BF16), hs["p_bd"])
        for hs in heads:
            hs["p_bd"] = block_diag(hs["p_pk"].astype(BF16))
            hs["t_pk"] = hs["t_pk"] + _dot(hs["t_pk"].astype(BF16), hs["p_bd"])
    for hs in heads:
        hs["av"] = _dot(hs["aak"], hs["vm"])
    for hs in heads:
        x = _dot(block_diag(hs["t_pk"].astype(BF16)),
                 jnp.concatenate([hs["am"], hs["av"]], axis=1).astype(BF16))
        hs["ph"], hs["qh"] = x[:, :LANES], x[:, LANES:]
    for hs in heads:
        hs["rp"] = hs["rm"] + _dot(hs["mrb"], hs["ph"].astype(BF16))
        hs["y0"] = _dot(hs["mrb"], hs["qh"].astype(BF16)) + _dot(hs["mrk"], hs["vm"])

    for p in range(N_PAIR):
        cols = slice(p * LANES, (p + 1) * LANES)
        ha, hb = heads[2 * p], heads[2 * p + 1]
        bp, kp, vp = bt[:, cols], kt[:, cols], v[:, cols]
        p_pair = ha["ph"] + hb["ph"]
        q_pair = ha["qh"] + hb["qh"]
        rp_ref[:, cols] = ha["rp"] + hb["rp"]
        y0_ref[:, cols] = ha["y0"] + hb["y0"]
        for c in range(N_CHUNK):
            rows = slice(c * CHUNK, (c + 1) * CHUNK)
            wc = jnp.exp(cw[(c + 1) * CHUNK - 1:(c + 1) * CHUNK, cols])
            bc = bp[rows].astype(BF16)
            gp = _dot_tn(p_pair[rows].astype(BF16), bc)
            hp = _dot_tn(jnp.concatenate([q_pair[rows], vp[rows]], axis=0).astype(BF16),
                         jnp.concatenate([bp[rows], kp[rows]], axis=0).astype(BF16))
            g_ref[0, c, p] = (eye_l + jnp.where(bd, gp, 0.0)) * wc
            h_ref[0, c, p] = jnp.where(bd, hp, 0.0) * wc


def _rwkv_prep(zr, vfirst, mu, par, wd, wa, wg, wvm, has_vmix, B, S):
    ns = S // TM
    T = B * S
    tile = lambda w: pl.BlockSpec((TM, w), lambda b, i: (b * ns + i, 0))
    gh_spec = pl.BlockSpec((1, N_CHUNK, N_PAIR, LANES, LANES), lambda b, i: (b * ns + i, 0, 0, 0, 0))
    gh_shape = jax.ShapeDtypeStruct((B * ns, N_CHUNK, N_PAIR, LANES, LANES), F32)
    wide = jax.ShapeDtypeStruct((T, BRANCH_WIDTH), F32)
    return pl.pallas_call(
        functools.partial(_rwkv_prep_body, has_vmix),
        grid=(B, ns),
        in_specs=[tile(RW_COLS),
                  pl.BlockSpec((8, RW_COLS), lambda b, i: (jnp.maximum((b * ns + i) * (TM // 8) - 1, 0), 0)),
                  tile(BRANCH_WIDTH), _const_spec(mu.shape), _const_spec(par.shape), _const_spec(wd.shape),
                  _const_spec(wa.shape), _const_spec(wg.shape), _const_spec(wvm.shape)],
        out_specs=[tile(BRANCH_WIDTH), tile(BRANCH_WIDTH), gh_spec, gh_spec,
                   tile(BRANCH_WIDTH), tile(BRANCH_WIDTH), tile(BRANCH_WIDTH)],
        out_shape=[wide, wide, gh_shape, gh_shape, wide, wide, wide],
        compiler_params=_cparams(("parallel", "parallel")),
        name="rwkv_prep",
    )(zr, zr, vfirst, mu, par, wd, wa, wg, wvm)


def _rwkv_scan_body(rp_ref, y0_ref, g_ref, h_ref, bonus_ref, gate_ref, ln_ref, o_ref, s_ref):
    i = pl.program_id(1)

    @pl.when(i == 0)
    def _():
        s_ref[...] = jnp.zeros_like(s_ref)

    states = [s_ref[p] for p in range(N_PAIR)]
    yc = [[] for _ in range(N_PAIR)]
    for c in range(N_CHUNK):
        rows = slice(c * CHUNK, (c + 1) * CHUNK)
        for p in range(N_PAIR):
            cols = slice(p * LANES, (p + 1) * LANES)
            s16 = states[p].astype(BF16)
            yc[p].append(_dot_nt(rp_ref[rows, cols].astype(BF16), s16) + y0_ref[rows, cols])
            states[p] = _dot(s16, g_ref[0, c, p].astype(BF16)) + h_ref[0, c, p]
    for p in range(N_PAIR):
        s_ref[p] = states[p]
    y = jnp.concatenate([jnp.concatenate(yc[p], axis=0) for p in range(N_PAIR)], axis=1)
    seg = ((lax.broadcasted_iota(jnp.int32, (LANES, LANES), 0) < HEAD_DIM)
           == (lax.broadcasted_iota(jnp.int32, (LANES, LANES), 1) < HEAD_DIM)).astype(BF16)
    mu = _seg_sum(y, seg) * (1.0 / HEAD_DIM)
    d = y - mu
    var = _seg_sum(d * d, seg) * (1.0 / HEAD_DIM)
    yn = d * lax.rsqrt(var + LNX_EPS) * ln_ref[0:1, :] + ln_ref[1:2, :]
    o_ref[...] = ((yn + bonus_ref[...]) * gate_ref[...]).astype(BF16)


def _rwkv_scan(rp, y0, g, h, bonus, gate, ln, B, S):
    ns = S // TM
    tile = pl.BlockSpec((TM, BRANCH_WIDTH), lambda b, i: (b * ns + i, 0))
    gh_spec = pl.BlockSpec((1, N_CHUNK, N_PAIR, LANES, LANES), lambda b, i: (b * ns + i, 0, 0, 0, 0))
    return pl.pallas_call(
        _rwkv_scan_body,
        grid=(B, ns),
        in_specs=[tile, tile, gh_spec, gh_spec, tile, tile, _const_spec(ln.shape)],
        out_specs=tile,
        out_shape=jax.ShapeDtypeStruct((B * S, BRANCH_WIDTH), BF16),
        scratch_shapes=[pltpu.VMEM((N_PAIR, LANES, LANES), F32)],
        compiler_params=_cparams(("parallel", "arbitrary")),
        name="rwkv_scan",
    )(rp, y0, g, h, bonus, gate, ln)


def _merge_body(x_ref, gate_ref, ya_ref, yr_ref, yp_ref, wb_ref, wo_ref, g2_ref, rw_ref, rb_ref,
                x1_ref, h2_ref, eid_ref, wt_ref, cnt_ref):
    merged = jnp.zeros((TM, D_MODEL), F32)
    for bi, y_ref in enumerate((ya_ref, yr_ref, yp_ref)):
        merged = merged + gate_ref[:, bi * D_MODEL:(bi + 1) * D_MODEL] * _dot(y_ref[...], wb_ref[bi])
    x1 = x_ref[...] + _dot(merged.astype(BF16), wo_ref[...])
    x1_ref[...] = x1
    h2 = _rms(x1, g2_ref[...])
    h2_ref[...] = h2
    hi = h2.astype(BF16)
    lo = (h2 - hi.astype(F32)).astype(BF16)
    logits = _dot(hi, rw_ref[0]) + _dot(lo, rw_ref[0]) + _dot(hi, rw_ref[1]) + rb_ref[...]
    lane = lax.broadcasted_iota(jnp.int32, (TM, LANES), 1)
    ninf = -jnp.inf
    glog = jnp.where(lane < N_GROUPS, logits, ninf)
    gmax = jnp.max(glog, axis=-1, keepdims=True)
    grp = jnp.min(jnp.where(glog == gmax, lane, LANES), axis=-1, keepdims=True)
    gprob = 1.0 / jnp.sum(jnp.exp(glog - gmax), axis=-1, keepdims=True)
    e_lane = lane - N_EXPERTS
    in_grp = (e_lane >= grp * EXPERTS_PER_GROUP) & (e_lane < (grp + 1) * EXPERTS_PER_GROUP)
    elog = jnp.where(in_grp, logits, ninf)
    v1 = jnp.max(elog, axis=-1, keepdims=True)
    i1 = jnp.min(jnp.where(elog == v1, e_lane, LANES), axis=-1, keepdims=True)
    elog = jnp.where(e_lane == i1, ninf, elog)
    v2 = jnp.max(elog, axis=-1, keepdims=True)
    i2 = jnp.min(jnp.where(elog == v2, e_lane, LANES), axis=-1, keepdims=True)
    e2 = jnp.exp(v2 - v1)
    den = 1.0 + e2
    eid_ref[...] = jnp.concatenate([i1, i2], axis=1)
    wt_ref[...] = jnp.concatenate([gprob / den, gprob * e2 / den], axis=1)
    hit = ((lane == i1) | (lane == i2)).astype(F32)
    cnt_ref[0] = jnp.sum(hit, axis=0, keepdims=True).astype(jnp.int32)


def _merge(xf, gates, ya, yr, yp, wb, wo, g2, rw, rb):
    T = xf.shape[0]
    nt = T // TM
    row = lambda w: pl.BlockSpec((TM, w), lambda i: (i, 0))
    return pl.pallas_call(
        _merge_body,
        grid=(nt,),
        in_specs=[row(D_MODEL), row(GATE_COLS), row(BRANCH_WIDTH), row(BRANCH_WIDTH), row(BRANCH_WIDTH),
                  _const_spec(wb.shape), _const_spec(wo.shape), _const_spec(g2.shape), _const_spec(rw.shape),
                  _const_spec(rb.shape)],
        out_specs=[row(D_MODEL), row(D_MODEL), row(EXPERT_TOPK), row(EXPERT_TOPK),
                   pl.BlockSpec((1, 1, LANES), lambda i: (i, 0, 0))],
        out_shape=[jax.ShapeDtypeStruct((T, D_MODEL), F32), jax.ShapeDtypeStruct((T, D_MODEL), F32),
                   jax.ShapeDtypeStruct((T, EXPERT_TOPK), jnp.int32), jax.ShapeDtypeStruct((T, EXPERT_TOPK), F32),
                   jax.ShapeDtypeStruct((nt, 1, LANES), jnp.int32)],
        compiler_params=_cparams(("parallel",)),
        name="merge",
    )(xf, gates, ya, yr, yp, wb, wo, g2, rw, rb)


def _lane_excl_cumsum(x):
    lane = lax.broadcasted_iota(jnp.int32, x.shape, 1)
    inc = x
    sh = 1
    while sh < LANES:
        inc = inc + jnp.where(lane >= sh, pltpu.roll(inc, sh, axis=1), 0)
        sh *= 2
    return inc - x


def _dest_body(eid_ref, cnt_ref, dest_ref, pend_ref, base_ref):
    i = pl.program_id(0)

    @pl.when(i == 0)
    def _():
        total = jnp.sum(cnt_ref[...], axis=0)
        shift = MOE_BLOCK.bit_length() - 1
        padded = jnp.left_shift(jnp.right_shift(total + (MOE_BLOCK - 1), shift), shift)
        pstart = _lane_excl_cumsum(padded)
        base_ref[...] = pstart
        pend_ref[...] = pstart + padded

    lane = lax.broadcasted_iota(jnp.int32, (TM, LANES), 1)
    e0, e1 = eid_ref[:, 0:1], eid_ref[:, 1:2]
    oh0, oh1 = lane == e0, lane == e1
    both = (oh0 | oh1).astype(BF16)
    lower = (lax.broadcasted_iota(jnp.int32, (TM, TM), 0)
             > lax.broadcasted_iota(jnp.int32, (TM, TM), 1)).astype(BF16)
    pos = _dot(lower, both) + base_ref[...].astype(F32)
    d0 = jnp.sum(jnp.where(oh0, pos, 0.0), axis=-1, keepdims=True)
    d1 = jnp.sum(jnp.where(oh1, pos, 0.0), axis=-1, keepdims=True)
    dest_ref[...] = jnp.concatenate([d0, d1], axis=1).astype(jnp.int32)
    base_ref[...] = base_ref[...] + jnp.sum(both.astype(F32), axis=0, keepdims=True).astype(jnp.int32)


def _moe_dest(eid, cnt):
    T = eid.shape[0]
    nt = T // TM
    return pl.pallas_call(
        _dest_body,
        grid=(nt,),
        in_specs=[pl.BlockSpec((TM, EXPERT_TOPK), lambda i: (i, 0)),
                  pl.BlockSpec((nt, 1, LANES), lambda i: (0, 0, 0))],
        out_specs=[pl.BlockSpec((TM, EXPERT_TOPK), lambda i: (i, 0)),
                   pl.BlockSpec((1, LANES), lambda i: (0, 0))],
        out_shape=[jax.ShapeDtypeStruct((T, EXPERT_TOPK), jnp.int32), jax.ShapeDtypeStruct((1, LANES), jnp.int32)],
        scratch_shapes=[pltpu.VMEM((1, LANES), jnp.int32)],
        compiler_params=_cparams(("arbitrary",)),
        name="moe_dest",
    )(eid, cnt)


def _scatter_body(dest_ref, h_ref, buf_in_ref, buf_ref, sem):
    del buf_in_ref
    i = pl.program_id(0)

    def issue(t, carry):
        for s in range(EXPERT_TOPK):
            d = dest_ref[(i * TM + t) * EXPERT_TOPK + s]
            pltpu.make_async_copy(h_ref.at[pl.ds(t, 1), :], buf_ref.at[pl.ds(d, 1), :], sem).start()
        return carry

    lax.fori_loop(0, TM, issue, 0, unroll=DMA_ISSUE_UNROLL)
    for _ in range(EXPERT_TOPK):
        pltpu.make_async_copy(h_ref, buf_ref.at[pl.ds(0, TM), :], sem).wait()


def _moe_scatter(dest_flat, h2, buf0):
    T = h2.shape[0]
    nt = T // TM
    return pl.pallas_call(
        _scatter_body,
        grid_spec=pltpu.PrefetchScalarGridSpec(
            num_scalar_prefetch=1, grid=(nt,),
            in_specs=[pl.BlockSpec((TM, D_MODEL), lambda i, d: (i, 0)), pl.BlockSpec(memory_space=pl.ANY)],
            out_specs=pl.BlockSpec(memory_space=pl.ANY),
            scratch_shapes=[pltpu.SemaphoreType.DMA(())]),
        out_shape=jax.ShapeDtypeStruct(buf0.shape, F32),
        input_output_aliases={2: 0},
        compiler_params=_cparams(("arbitrary",)),
        name="moe_scatter",
    )(dest_flat, h2, buf0)


def _expert_body(be_ref, x_ref, w1_ref, w3_ref, w2_ref, o_ref):
    del be_ref
    xb = x_ref[...].astype(BF16)
    h1 = _dot(xb, w1_ref[0])
    h3 = _dot(xb, w3_ref[0])
    hid = (h1 * _sigmoid(h1)) * h3
    o_ref[...] = _dot(hid.astype(BF16), w2_ref[0])


def _moe_experts(blk_e, buf, w1, w3, w2):
    R = buf.shape[0]
    nblk = R // MOE_BLOCK
    return pl.pallas_call(
        _expert_body,
        grid_spec=pltpu.PrefetchScalarGridSpec(
            num_scalar_prefetch=1, grid=(nblk,),
            in_specs=[pl.BlockSpec((MOE_BLOCK, D_MODEL), lambda b, e: (b, 0)),
                      pl.BlockSpec((1, D_MODEL, EXPERT_HIDDEN), lambda b, e: (e[b], 0, 0)),
                      pl.BlockSpec((1, D_MODEL, EXPERT_HIDDEN), lambda b, e: (e[b], 0, 0)),
                      pl.BlockSpec((1, EXPERT_HIDDEN, D_MODEL), lambda b, e: (e[b], 0, 0))],
            out_specs=pl.BlockSpec((MOE_BLOCK, D_MODEL), lambda b, e: (b, 0))),
        out_shape=jax.ShapeDtypeStruct((R, D_MODEL), F32),
        compiler_params=_cparams(("arbitrary",)),
        name="moe_experts",
    )(blk_e, buf, w1, w3, w2)


def _combine_body(final, dest_ref, x_ref, wt_ref, gf_ref, yb_ref, o_ref, g_ref, sem):
    i = pl.program_id(0)

    def issue_tile(tile, slot):
        def issue(t, carry):
            for s in range(EXPERT_TOPK):
                d = dest_ref[(tile * TM + t) * EXPERT_TOPK + s]
                pltpu.make_async_copy(yb_ref.at[pl.ds(d, 1), :], g_ref.at[slot, s, pl.ds(t, 1), :],
                                      sem.at[slot]).start()
            return carry

        lax.fori_loop(0, TM, issue, 0, unroll=DMA_ISSUE_UNROLL)

    @pl.when(i == 0)
    def _():
        issue_tile(0, 0)

    @pl.when(i + 1 < pl.num_programs(0))
    def _():
        issue_tile(i + 1, (i + 1) % 2)

    slot = i % 2
    for s in range(EXPERT_TOPK):
        pltpu.make_async_copy(yb_ref.at[pl.ds(0, TM), :], g_ref.at[slot, s], sem.at[slot]).wait()
    wt = wt_ref[...]
    x = x_ref[...] + (g_ref[slot, 0] * wt[:, 0:1] + g_ref[slot, 1] * wt[:, 1:2])
    if final:
        x = _rms(x, gf_ref[...])
    o_ref[...] = x


def _moe_combine(dest_flat, x1, wts, gf, yb, final):
    T = x1.shape[0]
    nt = T // TM
    return pl.pallas_call(
        functools.partial(_combine_body, final),
        grid_spec=pltpu.PrefetchScalarGridSpec(
            num_scalar_prefetch=1, grid=(nt,),
            in_specs=[pl.BlockSpec((TM, D_MODEL), lambda i, d: (i, 0)),
                      pl.BlockSpec((TM, EXPERT_TOPK), lambda i, d: (i, 0)),
                      pl.BlockSpec((1, D_MODEL), lambda i, d: (0, 0)),
                      pl.BlockSpec(memory_space=pl.ANY)],
            out_specs=pl.BlockSpec((TM, D_MODEL), lambda i, d: (i, 0)),
            scratch_shapes=[pltpu.VMEM((2, EXPERT_TOPK, TM, D_MODEL), F32), pltpu.SemaphoreType.DMA((2,))]),
        out_shape=jax.ShapeDtypeStruct((T, D_MODEL), F32),
        compiler_params=_cparams(("arbitrary",)),
        name="moe_combine",
    )(dest_flat, x1, wts, gf, yb)


def _pad_cols(w, n):
    return jnp.pad(w, ((0, 0), (0, n - w.shape[1])))


def _rwkv_cols(w_rw, w_vm):
    W = BRANCH_WIDTH
    o1, o2, o3, o4 = 3 * W, 3 * W + DECAY_LORA, 3 * W + DECAY_LORA + ICLR_LORA, 3 * W + DECAY_LORA + ICLR_LORA + GATE_LORA
    return jnp.concatenate([w_rw[:, :o1], _pad_cols(w_rw[:, o1:o2], LANES), _pad_cols(w_rw[:, o2:o3], LANES),
                            w_rw[:, o3:o4], _pad_cols(w_vm, LANES)], axis=1)


def _pad_rows(w, n):
    return jnp.pad(w, ((0, n - w.shape[0]), (0, 0)))


def kernel(x, norm1_g, w_in, w_vmix, rwkv_mu, vmix_mu, w0, w_decay_up, a0, w_iclr_up, w_gate_up, k_k, k_a, r_k,
           lnx_g, lnx_b, v0, w_vmix_up, pool_w, pool_scale, w_branch, w_out, norm2_g, router_grp_w, router_grp_b,
           router_exp_w, router_exp_b, exp_w1, exp_w3, exp_w2, final_norm_g):
    B, S, D = x.shape
    assert D == D_MODEL and S % (TM * ATTN_GROUP) == 0
    T = B * S
    ns = S // TM
    nb = ns
    xf = x.reshape(T, D)

    slopes = jnp.asarray([2.0 ** (-8.0 * (h + 1) / ATTN_HEADS) * LOG2E for h in range(ATTN_HEADS)], F32)
    slopes = slopes.reshape(N_PAIR, 2, 1)
    s_hi = slopes.astype(BF16).astype(F32)
    s_lo = slopes - s_hi
    slope_cols = jnp.zeros((N_PAIR, 2, TM, LANES), F32)
    for col, val in enumerate((s_hi, s_lo, s_hi * TM, s_lo * TM)):
        slope_cols = slope_cols.at[:, :, :, col].set(val)
    slope_cols = slope_cols.astype(BF16)
    key_row = jnp.arange(ATTN_GROUP * TM, dtype=jnp.int32)
    in_blk, blk_no = (key_row % TM).astype(F32), (key_row // TM).astype(F32)
    ccol = jnp.zeros((ATTN_GROUP * TM, LANES), F32)
    for col, val in enumerate((in_blk, in_blk, blk_no, blk_no)):
        ccol = ccol.at[:, col].set(val)
    ccol = ccol.astype(BF16)

    n_assign = T * EXPERT_TOPK
    R = -(-(n_assign + N_EXPERTS * (MOE_BLOCK - 1)) // MOE_BLOCK) * MOE_BLOCK
    nblk = R // MOE_BLOCK

    v_first = jnp.zeros((T, BRANCH_WIDTH), F32)
    for l in range(DEPTH):
        wl = w_in[l]
        wa = wl[:, :2 * BRANCH_WIDTH].astype(BF16)
        wvt = wl[:, 2 * BRANCH_WIDTH:ATTN_COLS].T.astype(BF16)
        wg =wl[:, GATE_OFF:GATE_OFF + GATE_COLS].astype(BF16)
        wp = wl[:, POOL_OFF:POOL_OFF + BRANCH_WIDTH].astype(BF16)
        has_vmix = l > 0
        w_vm = w_vmix[l - 1] if has_vmix else jnp.zeros((D, VMIX_LORA), F32)
        mu_vm = vmix_mu[l - 1] if has_vmix else jnp.zeros((VMIX_LORA,), F32)
        wr = _rwkv_cols(wl[:, RWKV_OFF:], w_vm).astype(BF16)
        mu = _rwkv_cols(rwkv_mu[l][None, :], mu_vm[None, :])
        q, k, vt, gates, zp, zr, kmean = _inproj(xf, norm1_g[l][None, :], wa, wvt, wg, wp, wr)

        ids = _moba_select(q, kmean.reshape(B, nb, BRANCH_WIDTH), B, S)
        y_attn = _moba_attn(ids, q, k, vt, slope_cols, ccol, B, S)

        par = jnp.stack([w0[l], a0[l], k_k[l], k_a[l], r_k[l].reshape(-1),
                         v0[l - 1] if has_vmix else jnp.zeros((BRANCH_WIDTH,), F32),
                         jnp.zeros((BRANCH_WIDTH,), F32), jnp.zeros((BRANCH_WIDTH,), F32)])
        wvm_up = _pad_rows(w_vmix_up[l - 1], LANES) if has_vmix else jnp.zeros((LANES, BRANCH_WIDTH), F32)
        rp, y0, gm, hm, bonus, rgate, v_cur = _rwkv_prep(
            zr, v_first, mu, par, _pad_rows(w_decay_up[l], LANES), _pad_rows(w_iclr_up[l], LANES), w_gate_up[l],
            wvm_up, has_vmix, B, S)
        if l == 0:
            v_first = v_cur
        y_rwkv = _rwkv_scan(rp, y0, gm, hm, bonus, rgate, jnp.stack([lnx_g[l], lnx_b[l]]), B, S)

        y_pool = _pool(zp, pool_w[l].astype(BF16), pool_scale[l][None, :], B, S)

        rw = jnp.zeros((D, LANES), F32).at[:, :N_GROUPS].set(router_grp_w[l]).at[:, N_EXPERTS:2 * N_EXPERTS].set(router_exp_w[l])
        rw_hi = rw.astype(BF16)
        rw_lo = (rw - rw_hi.astype(F32)).astype(BF16)
        rb = jnp.zeros((1, LANES), F32).at[0, :N_GROUPS].set(router_grp_b[l]).at[0, N_EXPERTS:2 * N_EXPERTS].set(router_exp_b[l])
        x1, h2, eid, wts, cnt = _merge(xf, gates, y_attn, y_rwkv, y_pool, w_branch[l].astype(BF16),
                                       w_out[l].astype(BF16), norm2_g[l][None, :], jnp.stack([rw_hi, rw_lo]), rb)

        dest, pends = _moe_dest(eid, cnt)
        dest_flat = dest.reshape(-1)
        blk_row = jnp.arange(nblk, dtype=jnp.int32)[:, None] * MOE_BLOCK
        blk_e = jnp.minimum(jnp.sum((pends[0, :N_EXPERTS][None, :] <= blk_row).astype(jnp.int32), axis=1),
                            N_EXPERTS - 1)
        buf = _moe_scatter(dest_flat, h2, jnp.zeros((R, D), F32))
        yb = _moe_experts(blk_e, buf, exp_w1[l].astype(BF16), exp_w3[l].astype(BF16), exp_w2[l].astype(BF16))
        xf = _moe_combine(dest_flat, x1, wts, final_norm_g[None, :], yb, l == DEPTH - 1)
    return xf.reshape(B, S, D)
```

```python
import functools

import jax
import jax.numpy as jnp
from jax import lax
from jax.experimental import pallas as pl
from jax.experimental.pallas import tpu as pltpu

F32 = jnp.float32
BF16 = jnp.bfloat16
HIGHEST = lax.Precision.HIGHEST
LOG2E = 1.4426950408889634

D_MODEL = 1024
DEPTH = 2
BRANCH_WIDTH = 512
N_BRANCH = 3
ATTN_HEADS = 8
HEAD_DIM = 64
MOBA_BLOCK = 256
MOBA_TOPK = 3
DECAY_LORA = 64
ICLR_LORA = 64
GATE_LORA = 128
VMIX_LORA = 32
LNX_EPS = 64e-5
POOL_WINDOWS = (2, 4, 8, 16)
POOL_GROUP_DIM = 128
N_GROUPS = 4
EXPERTS_PER_GROUP = 8
N_EXPERTS = 32
EXPERT_TOPK = 2
EXPERT_HIDDEN = 512
RMS_EPS = 1e-6
ATTN_COLS = 3 * BRANCH_WIDTH
GATE_OFF = ATTN_COLS
GATE_COLS = N_BRANCH * D_MODEL
POOL_OFF = GATE_OFF + GATE_COLS
RWKV_OFF = POOL_OFF + BRANCH_WIDTH

LANES = 128
TM = 256
CHUNK = 64
N_CHUNK = TM // CHUNK
N_PAIR = ATTN_HEADS // 2
RW_COLS = 2048
POOL_HALO = 16
ATTN_GROUP = 4
MOE_BLOCK = 256
DMA_ISSUE_UNROLL = 8
V_ROWS = 80
VMEM_LIMIT = 56 * 1024 * 1024


def _cparams(sem):
    return pltpu.CompilerParams(dimension_semantics=sem, vmem_limit_bytes=VMEM_LIMIT)


def _const_spec(shape):
    nd = len(shape)
    return pl.BlockSpec(shape, lambda *a: (0,) * nd, pipeline_mode=pl.Buffered(1))


def _sigmoid(x):
    return 1.0 / (1.0 + jnp.exp(-x))


def _rms(x, g):
    return x * lax.rsqrt(jnp.mean(x * x, axis=-1, keepdims=True) + RMS_EPS) * g


def _split3(x):
    h1 = x.astype(BF16)
    r1 = x - h1.astype(F32)
    h2 = r1.astype(BF16)
    h3 = (r1 - h2.astype(F32)).astype(BF16)
    return h1, h2, h3


def _dot(a, b):
    return jnp.dot(a, b, preferred_element_type=F32)


def _dot_nt(a, b):
    return lax.dot_general(a, b, (((1,), (1,)), ((), ())), preferred_element_type=F32)


def _dot_tn(a, b):
    return lax.dot_general(a, b, (((0,), (0,)), ((), ())), preferred_element_type=F32)


def _inproj_body(x_ref, g_ref, wa_ref, wvt_ref, wg_ref, wp_ref, wr_ref,
                 q_ref, k_ref, vt_ref, gate_ref, zp_ref, zr_ref, km_ref):
    hb = _rms(x_ref[...], g_ref[...]).astype(BF16)
    za = _dot(hb, wa_ref[...])
    q_ref[...] = (za[:, :BRANCH_WIDTH] * (HEAD_DIM ** -0.5 * LOG2E)).astype(BF16)
    k = za[:, BRANCH_WIDTH:]
    k_ref[...] = k.astype(BF16)
    vt = _dot_nt(wvt_ref[...], hb)
    ones = jnp.ones((V_ROWS - HEAD_DIM, TM), BF16)
    for hd in range(ATTN_HEADS):
        vt_ref[0, hd // 2, hd % 2, 0:HEAD_DIM, :] = vt[hd * HEAD_DIM:(hd + 1) * HEAD_DIM].astype(BF16)
        vt_ref[0, hd // 2, hd % 2, HEAD_DIM:, :] = ones
    km_ref[0] = jnp.mean(k, axis=0, keepdims=True)
    gate_ref[...] = _sigmoid(_dot(hb, wg_ref[...]))
    zp_ref[...] = _dot(hb, wp_ref[...])
    zr_ref[...] = _dot(hb, wr_ref[...])


def _inproj(xf, g, wa, wvt, wg, wp, wr):
    T = xf.shape[0]
    nt = T // TM
    row = lambda w: pl.BlockSpec((TM, w), lambda i: (i, 0))
    return pl.pallas_call(
        _inproj_body,
        grid=(nt,),
        in_specs=[row(D_MODEL), _const_spec((1, D_MODEL)), _const_spec(wa.shape), _const_spec(wvt.shape),
                  _const_spec(wg.shape), _const_spec(wp.shape), _const_spec(wr.shape)],
        out_specs=[row(BRANCH_WIDTH), row(BRANCH_WIDTH),
                   pl.BlockSpec((1, N_PAIR, 2, V_ROWS, TM), lambda i: (i, 0, 0, 0, 0)),
                   row(GATE_COLS), row(BRANCH_WIDTH), row(RW_COLS),
                   pl.BlockSpec((1, 1, BRANCH_WIDTH), lambda i: (i, 0, 0))],
        out_shape=[jax.ShapeDtypeStruct((T, BRANCH_WIDTH), BF16), jax.ShapeDtypeStruct((T, BRANCH_WIDTH), BF16),
                   jax.ShapeDtypeStruct((nt, N_PAIR, 2, V_ROWS, TM), BF16),
                   jax.ShapeDtypeStruct((T, GATE_COLS), F32), jax.ShapeDtypeStruct((T, BRANCH_WIDTH), F32),
                   jax.ShapeDtypeStruct((T, RW_COLS), F32), jax.ShapeDtypeStruct((nt, 1, BRANCH_WIDTH), F32)],
        compiler_params=_cparams(("parallel",)),
        name="inproj",
    )(xf, g, wa, wvt, wg, wp, wr)


def _select_body(q_ref, km_ref, o_ref):
    i = pl.program_id(1)
    nb = km_ref.shape[1]
    lane = lax.broadcasted_iota(jnp.int32, (1, LANES), 1)
    blk = lax.broadcasted_iota(jnp.int32, (nb, TM), 0)
    for p in range(N_PAIR):
        q = q_ref[:, p * LANES:(p + 1) * LANES]
        km = km_ref[0, :, p * LANES:(p + 1) * LANES]
        rows = []
        for half in range(2):
            in_head = (lane < HEAD_DIM) if half == 0 else (lane >= HEAD_DIM)
            g = sum(_dot_nt(part, q) for part in _split3(jnp.where(in_head, km, 0.0)))
            g = jnp.where(blk < i, g, -jnp.inf)
            for _ in range(MOBA_TOPK):
                mx = jnp.max(g, axis=0, keepdims=True)
                idx = jnp.min(jnp.where(g == mx, blk, nb), axis=0, keepdims=True)
                rows.append(jnp.where(mx > -jnp.inf, idx, -1))
                g = jnp.where(blk == idx, -jnp.inf, g)
        rows += [jnp.full((1, TM), -1, jnp.int32)] * 2
        o_ref[p] = jnp.concatenate(rows, axis=0)


def _moba_select(q, kmean, B, S):
    ns = S // TM
    nb = kmean.shape[1]
    return pl.pallas_call(
        _select_body,
        grid=(B, ns),
        in_specs=[pl.BlockSpec((TM, BRANCH_WIDTH), lambda b, i: (b * ns + i, 0)),
                  pl.BlockSpec((1, nb, BRANCH_WIDTH), lambda b, i: (b, 0, 0))],
        out_specs=pl.BlockSpec((N_PAIR, 8, TM), lambda b, i: (b, 0, i)),
        out_shape=jax.ShapeDtypeStruct((B * N_PAIR, 8, S), jnp.int32),
        compiler_params=_cparams(("parallel", "parallel")),
        name="moba_select",
    )(q, kmean)


def _attn_body(ids_ref, q_ref, k_ref, vt_ref, slope_ref, ccol_ref, o_ref, qa_ref, m_ref, acc_ref, sa_ref, sb_ref,
               mxa_ref, mxb_ref, pa_ref, pb_ref):
    i = pl.program_id(2)
    lane = lax.broadcasted_iota(jnp.int32, (1, LANES), 1)
    first = lane < HEAD_DIM
    q = q_ref[...]
    zero = jnp.zeros_like(q)
    qa_ref[0, :, 0:LANES] = jnp.where(first, q, zero)
    qa_ref[1, :, 0:LANES] = jnp.where(first, zero, q)
    qa_ref[:, :, LANES:] = slope_ref[0]
    ids = ids_ref[0]

    def scores(j0, nblk):
        kj = k_ref[pl.ds(pl.multiple_of(j0 * TM, TM), nblk * TM), :]
        kaug = jnp.concatenate([kj, ccol_ref[0:nblk * TM, :]], axis=1)
        return [_dot_nt(kaug, qa_ref[h]) for h in range(2)]

    def pv(h, p, j0, nblk):
        vaug = jnp.concatenate([vt_ref[j0 + u, 0, h] for u in range(nblk)], axis=1)
        return _dot(vaug, p)

    kidx = lax.broadcasted_iota(jnp.int32, (TM, TM), 0)
    qidx = lax.broadcasted_iota(jnp.int32, (TM, TM), 1)
    def own_block():
        for h, s in enumerate(scores(i, 1)):
            s = jnp.where(kidx <= qidx, s, -jnp.inf)
            m0 = jnp.max(s, axis=0, keepdims=True)
            m_ref[h] = m0
            acc_ref[h] = pv(h, jnp.exp2(s - m0).astype(BF16), i, 1)

    def score_block(g, u, buf, mx):
        kj = k_ref[pl.ds(pl.multiple_of((g * ATTN_GROUP + u) * TM, TM), TM), :]
        kaug = jnp.concatenate([kj, ccol_ref[u * TM:(u + 1) * TM, :]], axis=1)
        for h in range(2):
            s = _dot_nt(kaug, qa_ref[h])
            buf[h, u * TM:(u + 1) * TM, :] = s
            parts = [s[r:r + 8] for r in range(0, 64, 8)]
            for r in range(64, TM, 8):
                parts[(r // 8) % 8] = jnp.maximum(parts[(r // 8) % 8], s[r:r + 8])
            while len(parts) > 1:
                parts = [jnp.maximum(parts[n], parts[n + 1]) for n in range(0, len(parts), 2)]
            mx[h, u] = parts[0]

    def stage(g, cur, nxt, g_next):
        cur_s, cur_mx, cur_p = cur
        j0 = g * ATTN_GROUP
        dist = ((j0 - i) * TM).astype(F32)
        stats = []
        for h in range(2):
            slope = slope_ref[0, h, 0:1, 0:1].astype(F32) + slope_ref[0, h, 0:1, 1:2].astype(F32)
            c0 = slope * dist
            r0 = h * MOBA_TOPK
            sels, best = [], None
            for u in range(ATTN_GROUP):
                j = j0 + u
                sel = (ids[r0:r0 + 1] == j) | (ids[r0 + 1:r0 + 2] == j) | (ids[r0 + 2:r0 + 3] == j)
                cand = jnp.where(sel, cur_mx[h, u], -jnp.inf)
                best = cand if best is None else jnp.maximum(best, cand)
                sels.append(sel)
            m_old = m_ref[h]
            m_new = jnp.maximum(m_old, jnp.max(best, axis=0, keepdims=True) + c0)
            m_ref[h] = m_new
            stats.append((jnp.exp2(m_old - m_new), [jnp.where(sel, m_new - c0, jnp.inf) for sel in sels]))
        for u in range(ATTN_GROUP):
            if nxt is not None:
                score_block(g_next, u, nxt[0], nxt[1])
            for h in range(2):
                cur_p[h, u * TM:(u + 1) * TM, :] = jnp.exp2(cur_s[h, u * TM:(u + 1) * TM, :] - stats[h][1][u]).astype(BF16)
        for h in range(2):
            acc_ref[h] = acc_ref[h] * stats[h][0] + pv(h, cur_p[h], j0, ATTN_GROUP)

    n_group = (i + ATTN_GROUP - 1) // ATTN_GROUP
    buf_a, buf_b = (sa_ref, mxa_ref, pa_ref), (sb_ref, mxb_ref, pb_ref)

    for u in range(ATTN_GROUP):
        score_block(0, u, sa_ref, mxa_ref)
    own_block()

    def pair_body(t, carry):
        stage(2 * t, buf_a, buf_b, 2 * t + 1)
        stage(2 * t + 1, buf_b, buf_a, jnp.minimum(2 * t + 2, n_group - 1))
        return carry

    lax.fori_loop(0, n_group // 2, pair_body, 0)

    @pl.when(n_group % 2 == 1)
    def _():
        stage(n_group - 1, buf_a, None, None)

    acc_a, acc_b = acc_ref[0], acc_ref[1]
    out_t = jnp.concatenate([acc[:HEAD_DIM] / acc[HEAD_DIM:HEAD_DIM + 1] for acc in (acc_a, acc_b)], axis=0)
    o_ref[...] = out_t.T.astype(BF16)


def _moba_attn(ids, q, k, vt, slope_cols, ccol, B, S):
    ns = S // TM
    return pl.pallas_call(
        _attn_body,
        grid=(B, N_PAIR, ns),
        in_specs=[pl.BlockSpec((1, 8, TM), lambda b, p, i: (b * N_PAIR + p, 0, i)),
                  pl.BlockSpec((TM, LANES), lambda b, p, i: (b * ns + i, p)),
                  pl.BlockSpec((S, LANES), lambda b, p, i: (b, p)),
                  pl.BlockSpec((ns, 1, 2, V_ROWS, TM), lambda b, p, i: (b, p, 0, 0, 0)),
                  pl.BlockSpec((1, 2, TM, LANES), lambda b, p, i: (p, 0, 0, 0)),
                  pl.BlockSpec((ATTN_GROUP * TM, LANES), lambda b, p, i: (0, 0))],
        out_specs=pl.BlockSpec((TM, LANES), lambda b, p, i: (b * ns + i, p)),
        out_shape=jax.ShapeDtypeStruct((B * S, BRANCH_WIDTH), BF16),
        scratch_shapes=[pltpu.VMEM((2, TM, 2 * LANES), BF16), pltpu.VMEM((2, 1, TM), F32),
                        pltpu.VMEM((2, V_ROWS, TM), F32),
                        pltpu.VMEM((2, ATTN_GROUP * TM, TM), F32), pltpu.VMEM((2, ATTN_GROUP * TM, TM), F32),
                        pltpu.VMEM((2, ATTN_GROUP, 8, TM), F32), pltpu.VMEM((2, ATTN_GROUP, 8, TM), F32),
                        pltpu.VMEM((2, ATTN_GROUP * TM, TM), BF16), pltpu.VMEM((2, ATTN_GROUP * TM, TM), BF16)],
        compiler_params=_cparams(("parallel", "parallel", "arbitrary")),
        name="moba_attn",
    )(ids, q, k, vt, slope_cols, ccol)


def _pool_body(z_ref, halo_ref, w_ref, sc_ref, o_ref, ext_ref):
    i = pl.program_id(1)
    ext_ref[0:POOL_HALO, :] = jnp.where(i == 0, 0.0, halo_ref[...])
    ext_ref[POOL_HALO:, :] = z_ref[...]
    t1 = (i * TM + 1 + lax.broadcasted_iota(jnp.int32, (TM, 1), 0)).astype(F32)
    for gi, win in enumerate(POOL_WINDOWS):
        cols = slice(gi * POOL_GROUP_DIM, (gi + 1) * POOL_GROUP_DIM)
        acc = ext_ref[POOL_HALO:, cols]
        for d in range(1, win):
            acc = acc + ext_ref[POOL_HALO - d:POOL_HALO - d + TM, cols]
        p = acc / jnp.minimum(t1, float(win)) - ext_ref[POOL_HALO:, cols]
        y = _dot(p.astype(BF16), w_ref[gi])
        o_ref[:, cols] = (y * sc_ref[:, cols]).astype(BF16)


def _pool(zp, w, sc, B, S):
    ns = S // TM
    hb = TM // POOL_HALO
    return pl.pallas_call(
        _pool_body,
        grid=(B, ns),
        in_specs=[pl.BlockSpec((TM, BRANCH_WIDTH), lambda b, i: (b * ns + i, 0)),
                  pl.BlockSpec((POOL_HALO, BRANCH_WIDTH), lambda b, i: (jnp.maximum((b * ns + i) * hb - 1, 0), 0)),
                  _const_spec(w.shape), _const_spec(sc.shape)],
        out_specs=pl.BlockSpec((TM, BRANCH_WIDTH), lambda b, i: (b * ns + i, 0)),
        out_shape=jax.ShapeDtypeStruct((B * S, BRANCH_WIDTH), BF16),
        scratch_shapes=[pltpu.VMEM((POOL_HALO + TM, BRANCH_WIDTH), F32)],
        compiler_params=_cparams(("parallel", "parallel")),
        name="pool",
    )(zp, zp, w, sc)


def _seg_sum(x, seg):
    hi = x.astype(BF16)
    lo = (x - hi.astype(F32)).astype(BF16)
    outs = []
    for p in range(N_PAIR):
        cols = slice(p * LANES, (p + 1) * LANES)
        outs.append(_dot(hi[:, cols], seg) + _dot(lo[:, cols], seg))
    return jnp.concatenate(outs, axis=1)


def _rwkv_body(has_vmix, z_ref, halo_ref, vf_ref, mu_ref, par_ref, wd_ref, wa_ref, wg_ref, wvm_ref, ln_ref,
               o_ref, v_ref, s_ref):
    i = pl.program_id(1)

    @pl.when(i == 0)
    def _():
        s_ref[...] = jnp.zeros_like(s_ref)

    W = BRANCH_WIDTH
    z = z_ref[...]
    prev0 = jnp.where(i == 0, 0.0, halo_ref[7:8, :])
    row = lax.broadcasted_iota(jnp.int32, (TM, 1), 0)
    prev = jnp.where(row == 0, prev0, pltpu.roll(z, 1, axis=0))
    rw = z + (prev - z) * mu_ref[...]
    r, k, v = rw[:, 0:W], rw[:, W:2 * W], rw[:, 2 * W:3 * W]
    w_lo = rw[:, 3 * W:3 * W + LANES]
    a_lo = rw[:, 3 * W + LANES:3 * W + 2 * LANES]
    g_lo = rw[:, 3 * W + 2 * LANES:3 * W + 3 * LANES]
    vm_lo = rw[:, 3 * W + 3 * LANES:3 * W + 4 * LANES]
    w0, a0, k_k, k_a, r_k, v0 = (par_ref[n:n + 1, :] for n in range(6))
    hdot = functools.partial(jnp.dot, precision=HIGHEST, preferred_element_type=F32)

    u = w0 + hdot(jnp.tanh(w_lo), wd_ref[...])
    nu = -u
    softplus = jnp.maximum(nu, 0.0) + jnp.log(1.0 + jnp.exp(-jnp.abs(nu)))
    logw = -jnp.exp(-softplus - 0.5)
    if has_vmix:
        v = v + (vf_ref[...] - v) * _sigmoid(v0 + hdot(vm_lo, wvm_ref[...]))
    v_ref[...] = v
    a = _sigmoid(a0 + hdot(a_lo, wa_ref[...]))
    gate = hdot(_sigmoid(g_lo), wg_ref[...])

    lane = lax.broadcasted_iota(jnp.int32, (1, LANES), 1)
    first = lane < HEAD_DIM
    seg = ((lax.broadcasted_iota(jnp.int32, (LANES, LANES), 0) < HEAD_DIM)
           == (lax.broadcasted_iota(jnp.int32, (LANES, LANES), 1) < HEAD_DIM)).astype(BF16)
    kk = k * k_k
    kk = kk / jnp.maximum(jnp.sqrt(_seg_sum(kk * kk, seg)), 1e-12)
    k2 = k * (1.0 + (a - 1.0) * k_a)
    bonus = _seg_sum(r * k2 * r_k, seg) * v

    ri = lax.broadcasted_iota(jnp.int32, (TM, TM), 0)
    ci = lax.broadcasted_iota(jnp.int32, (TM, TM), 1)
    same = (ri // CHUNK) == (ci // CHUNK)
    incl = same & (ri >= ci)
    strict = same & (ri > ci)
    lmat = incl.astype(BF16)
    h1, h2, h3 = _split3(logw)
    cw = _dot(lmat, h1) + _dot(lmat, h2) + _dot(lmat, h3)
    e_neg = jnp.exp(-cw)
    at = (-kk) * jnp.exp(cw - logw)
    bt = (kk * a) * e_neg
    kt = k2 * e_neg
    rt = r * jnp.exp(cw)
    eye_pk = (lax.broadcasted_iota(jnp.int32, (CHUNK, TM), 0)
              == lax.broadcasted_iota(jnp.int32, (CHUNK, TM), 1) % CHUNK).astype(F32)
    bd = ((lax.broadcasted_iota(jnp.int32, (LANES, LANES), 0) < HEAD_DIM)
          == (lax.broadcasted_iota(jnp.int32, (LANES, LANES), 1) < HEAD_DIM))
    eye_l = (lax.broadcasted_iota(jnp.int32, (LANES, LANES), 0)
             == lax.broadcasted_iota(jnp.int32, (LANES, LANES), 1)).astype(F32)

    def block_diag(packed):
        return jnp.where(same, jnp.concatenate([packed] * N_CHUNK, axis=0), jnp.zeros((), packed.dtype))

    heads = []
    for hd in range(ATTN_HEADS):
        cols = slice((hd // 2) * LANES, (hd // 2 + 1) * LANES)
        mh = first if hd % 2 == 0 else jnp.logical_not(first)
        am = jnp.where(mh, at[:, cols], 0.0)
        rm = jnp.where(mh, rt[:, cols], 0.0)
        vm = jnp.where(mh, v[:, cols], 0.0).astype(BF16)
        bk = jnp.concatenate([bt[:, cols], kt[:, cols]], axis=0).astype(BF16)
        big = _dot_nt(jnp.concatenate([am, rm], axis=0).astype(BF16), bk)
        n = jnp.where(strict, big[:TM, :TM], 0.0)
        n_pk = n[0:CHUNK]
        for c in range(1, N_CHUNK):
            n_pk = n_pk + n[c * CHUNK:(c + 1) * CHUNK]
        heads.append(dict(
            am=am, rm=rm, vm=vm, t_pk=eye_pk + n_pk, p_pk=n_pk, p_bd=n.astype(BF16),
            aak=jnp.where(strict, big[:TM, TM:], 0.0).astype(BF16),
            mrb=jnp.where(incl, big[TM:, :TM], 0.0).astype(BF16),
            mrk=jnp.where(incl, big[TM:, TM:], 0.0).astype(BF16)))
    for _ in range(CHUNK.bit_length() - 2):
        for hs in heads:
            hs["p_pk"] = _dot(hs["p_pk"].astype(BF16), hs["p_bd"])
        for hs in heads:
            hs["p_bd"] = block_diag(hs["p_pk"].astype(BF16))
            hs["t_pk"] = hs["t_pk"] + _dot(hs["t_pk"].astype(BF16), hs["p_bd"])
    for hs in heads:
        hs["av"] = _dot(hs["aak"], hs["vm"])
    for hs in heads:
        x = _dot(block_diag(hs["t_pk"].astype(BF16)),
                 jnp.concatenate([hs["am"], hs["av"]], axis=1).astype(BF16))
        hs["ph"], hs["qh"] = x[:, :LANES], x[:, LANES:]
    for hs in heads:
        hs["rp"] = hs["rm"] + _dot(hs["mrb"], hs["ph"].astype(BF16))
        hs["y0"] = _dot(hs["mrb"], hs["qh"].astype(BF16)) + _dot(hs["mrk"], hs["vm"])

    pairs = []
    for p in range(N_PAIR):
        cols = slice(p * LANES, (p + 1) * LANES)
        ha, hb = heads[2 * p], heads[2 * p + 1]
        bp, kp, vp = bt[:, cols], kt[:, cols], v[:, cols]
        p_pair = ha["ph"] + hb["ph"]
        q_pair = ha["qh"] + hb["qh"]
        gmats, hmats = [], []
        for c in range(N_CHUNK):
            rows = slice(c * CHUNK, (c + 1) * CHUNK)
            wc = jnp.exp(cw[(c + 1) * CHUNK - 1:(c + 1) * CHUNK, cols])
            bc = bp[rows].astype(BF16)
            gp = _dot_tn(p_pair[rows].astype(BF16), bc)
            hp = _dot_tn(jnp.concatenate([q_pair[rows], vp[rows]], axis=0).astype(BF16),
                         jnp.concatenate([bp[rows], kp[rows]], axis=0).astype(BF16))
            gmats.append(((eye_l + jnp.where(bd, gp, 0.0)) * wc).astype(BF16))
            hmats.append(jnp.where(bd, hp, 0.0) * wc)
        pairs.append(dict(rp=(ha["rp"] + hb["rp"]).astype(BF16), y0=ha["y0"] + hb["y0"], g=gmats, h=hmats))

    states = [s_ref[p] for p in range(N_PAIR)]
    yc = [[] for _ in range(N_PAIR)]
    for c in range(N_CHUNK):
        rows = slice(c * CHUNK, (c + 1) * CHUNK)
        for p, pr in enumerate(pairs):
            s16 = states[p].astype(BF16)
            yc[p].append(_dot_nt(pr["rp"][rows], s16) + pr["y0"][rows])
            states[p] = _dot(s16, pr["g"][c]) + pr["h"][c]
    for p in range(N_PAIR):
        s_ref[p] = states[p]
    y = jnp.concatenate([jnp.concatenate(yc[p], axis=0) for p in range(N_PAIR)], axis=1)
    mean = _seg_sum(y, seg) * (1.0 / HEAD_DIM)
    d = y - mean
    var = _seg_sum(d * d, seg) * (1.0 / HEAD_DIM)
    yn = d * lax.rsqrt(var + LNX_EPS) * ln_ref[0:1, :] + ln_ref[1:2, :]
    o_ref[...] = ((yn + bonus) * gate).astype(BF16)


def _rwkv(zr, vfirst, mu, par, wd, wa, wg, wvm, ln, has_vmix, B, S):
    ns = S // TM
    T = B * S
    tile = lambda w: pl.BlockSpec((TM, w), lambda b, i: (b * ns + i, 0))
    return pl.pallas_call(
        functools.partial(_rwkv_body, has_vmix),
        grid=(B, ns),
        in_specs=[tile(RW_COLS),
                  pl.BlockSpec((8, RW_COLS), lambda b, i: (jnp.maximum((b * ns + i) * (TM // 8) - 1, 0), 0)),
                  tile(BRANCH_WIDTH), _const_spec(mu.shape), _const_spec(par.shape), _const_spec(wd.shape),
                  _const_spec(wa.shape), _const_spec(wg.shape), _const_spec(wvm.shape), _const_spec(ln.shape)],
        out_specs=[tile(BRANCH_WIDTH), tile(BRANCH_WIDTH)],
        out_shape=[jax.ShapeDtypeStruct((T, BRANCH_WIDTH), BF16), jax.ShapeDtypeStruct((T, BRANCH_WIDTH), F32)],
        scratch_shapes=[pltpu.VMEM((N_PAIR, LANES, LANES), F32)],
        compiler_params=_cparams(("parallel", "arbitrary")),
        name="rwkv",
    )(zr, zr, vfirst, mu, par, wd, wa, wg, wvm, ln)


def _merge_body(x_ref, gate_ref, ya_ref, yr_ref, yp_ref, wb_ref, wo_ref, g2_ref, rw_ref, rb_ref,
                x1_ref, h2_ref, eid_ref, wt_ref, cnt_ref):
    merged = jnp.zeros((TM, D_MODEL), F32)
    for bi, y_ref in enumerate((ya_ref, yr_ref, yp_ref)):
        merged = merged + gate_ref[:, bi * D_MODEL:(bi + 1) * D_MODEL] * _dot(y_ref[...], wb_ref[bi])
    x1 = x_ref[...] + _dot(merged.astype(BF16), wo_ref[...])
    x1_ref[...] = x1
    h2 = _rms(x1, g2_ref[...])
    h2_ref[...] = h2
    hi = h2.astype(BF16)
    lo = (h2 - hi.astype(F32)).astype(BF16)
    logits = _dot(hi, rw_ref[0]) + _dot(lo, rw_ref[0]) + _dot(hi, rw_ref[1]) + rb_ref[...]
    lane = lax.broadcasted_iota(jnp.int32, (TM, LANES), 1)
    ninf = -jnp.inf
    glog = jnp.where(lane < N_GROUPS, logits, ninf)
    gmax = jnp.max(glog, axis=-1, keepdims=True)
    grp = jnp.min(jnp.where(glog == gmax, lane, LANES), axis=-1, keepdims=True)
    gprob = 1.0 / jnp.sum(jnp.exp(glog - gmax), axis=-1, keepdims=True)
    e_lane = lane - N_EXPERTS
    in_grp = (e_lane >= grp * EXPERTS_PER_GROUP) & (e_lane < (grp + 1) * EXPERTS_PER_GROUP)
    elog = jnp.where(in_grp, logits, ninf)
    v1 = jnp.max(elog, axis=-1, keepdims=True)
    i1 = jnp.min(jnp.where(elog == v1, e_lane, LANES), axis=-1, keepdims=True)
    elog = jnp.where(e_lane == i1, ninf, elog)
    v2 = jnp.max(elog, axis=-1, keepdims=True)
    i2 = jnp.min(jnp.where(elog == v2, e_lane, LANES), axis=-1, keepdims=True)
    e2 = jnp.exp(v2 - v1)
    den = 1.0 + e2
    eid_ref[...] = jnp.concatenate([i1, i2], axis=1)
    wt_ref[...] = jnp.concatenate([gprob / den, gprob * e2 / den], axis=1)
    hit = ((lane == i1) | (lane == i2)).astype(F32)
    cnt_ref[0] = jnp.sum(hit, axis=0, keepdims=True).astype(jnp.int32)


def _merge(xf, gates, ya, yr, yp, wb, wo, g2, rw, rb):
    T = xf.shape[0]
    nt = T // TM
    row = lambda w: pl.BlockSpec((TM, w), lambda i: (i, 0))
    return pl.pallas_call(
        _merge_body,
        grid=(nt,),
        in_specs=[row(D_MODEL), row(GATE_COLS), row(BRANCH_WIDTH), row(BRANCH_WIDTH), row(BRANCH_WIDTH),
                  _const_spec(wb.shape), _const_spec(wo.shape), _const_spec(g2.shape), _const_spec(rw.shape),
                  _const_spec(rb.shape)],
        out_specs=[row(D_MODEL), row(D_MODEL), row(EXPERT_TOPK), row(EXPERT_TOPK),
                   pl.BlockSpec((1, 1, LANES), lambda i: (i, 0, 0))],
        out_shape=[jax.ShapeDtypeStruct((T, D_MODEL), F32), jax.ShapeDtypeStruct((T, D_MODEL), F32),
                   jax.ShapeDtypeStruct((T, EXPERT_TOPK), jnp.int32), jax.ShapeDtypeStruct((T, EXPERT_TOPK), F32),
                   jax.ShapeDtypeStruct((nt, 1, LANES), jnp.int32)],
        compiler_params=_cparams(("parallel",)),
        name="merge",
    )(xf, gates, ya, yr, yp, wb, wo, g2, rw, rb)


def _lane_excl_cumsum(x):
    lane = lax.broadcasted_iota(jnp.int32, x.shape, 1)
    inc = x
    sh = 1
    while sh < LANES:
        inc = inc + jnp.where(lane >= sh, pltpu.roll(inc, sh, axis=1), 0)
        sh *= 2
    return inc - x


def _dest_body(eid_ref, cnt_ref, dest_ref, pend_ref, base_ref):
    i = pl.program_id(0)

    @pl.when(i == 0)
    def _():
        total = jnp.sum(cnt_ref[...], axis=0)
        shift = MOE_BLOCK.bit_length() - 1
        padded = jnp.left_shift(jnp.right_shift(total + (MOE_BLOCK - 1), shift), shift)
        pstart = _lane_excl_cumsum(padded)
        base_ref[...] = pstart
        pend_ref[...] = pstart + padded

    lane = lax.broadcasted_iota(jnp.int32, (TM, LANES), 1)
    e0, e1 = eid_ref[:, 0:1], eid_ref[:, 1:2]
    oh0, oh1 = lane == e0, lane == e1
    both = (oh0 | oh1).astype(BF16)
    lower = (lax.broadcasted_iota(jnp.int32, (TM, TM), 0)
             > lax.broadcasted_iota(jnp.int32, (TM, TM), 1)).astype(BF16)
    pos = _dot(lower, both) + base_ref[...].astype(F32)
    d0 = jnp.sum(jnp.where(oh0, pos, 0.0), axis=-1, keepdims=True)
    d1 = jnp.sum(jnp.where(oh1, pos, 0.0), axis=-1, keepdims=True)
    dest_ref[...] = jnp.concatenate([d0, d1], axis=1).astype(jnp.int32)
    base_ref[...] = base_ref[...] + jnp.sum(both.astype(F32), axis=0, keepdims=True).astype(jnp.int32)


def _moe_dest(eid, cnt):
    T = eid.shape[0]
    nt = T // TM
    return pl.pallas_call(
        _dest_body,
        grid=(nt,),
        in_specs=[pl.BlockSpec((TM, EXPERT_TOPK), lambda i: (i, 0)),
                  pl.BlockSpec((nt, 1, LANES), lambda i: (0, 0, 0))],
        out_specs=[pl.BlockSpec((TM, EXPERT_TOPK), lambda i: (i, 0)),
                   pl.BlockSpec((1, LANES), lambda i: (0, 0))],
        out_shape=[jax.ShapeDtypeStruct((T, EXPERT_TOPK), jnp.int32), jax.ShapeDtypeStruct((1, LANES), jnp.int32)],
        scratch_shapes=[pltpu.VMEM((1, LANES), jnp.int32)],
        compiler_params=_cparams(("arbitrary",)),
        name="moe_dest",
    )(eid, cnt)


def _scatter_body(dest_ref, h_ref, buf_in_ref, buf_ref, sem):
    del buf_in_ref
    i = pl.program_id(0)

    def issue(t, carry):
        for s in range(EXPERT_TOPK):
            d = dest_ref[(i * TM + t) * EXPERT_TOPK + s]
            pltpu.make_async_copy(h_ref.at[pl.ds(t, 1), :], buf_ref.at[pl.ds(d, 1), :], sem).start()
        return carry

    lax.fori_loop(0, TM, issue, 0, unroll=DMA_ISSUE_UNROLL)
    for _ in range(EXPERT_TOPK):
        pltpu.make_async_copy(h_ref, buf_ref.at[pl.ds(0, TM), :], sem).wait()


def _moe_scatter(dest_flat, h2, buf0):
    T = h2.shape[0]
    nt = T // TM
    return pl.pallas_call(
        _scatter_body,
        grid_spec=pltpu.PrefetchScalarGridSpec(
            num_scalar_prefetch=1, grid=(nt,),
            in_specs=[pl.BlockSpec((TM, D_MODEL), lambda i, d: (i, 0)), pl.BlockSpec(memory_space=pl.ANY)],
            out_specs=pl.BlockSpec(memory_space=pl.ANY),
            scratch_shapes=[pltpu.SemaphoreType.DMA(())]),
        out_shape=jax.ShapeDtypeStruct(buf0.shape, F32),
        input_output_aliases={2: 0},
        compiler_params=_cparams(("arbitrary",)),
        name="moe_scatter",
    )(dest_flat, h2, buf0)


def _expert_body(be_ref, x_ref, w1_ref, w3_ref, w2_ref, o_ref):
    del be_ref
    xb = x_ref[...].astype(BF16)
    h1 = _dot(xb, w1_ref[0])
    h3 = _dot(xb, w3_ref[0])
    hid = (h1 * _sigmoid(h1)) * h3
    o_ref[...] = _dot(hid.astype(BF16), w2_ref[0])


def _moe_experts(blk_e, buf, w1, w3, w2):
    R = buf.shape[0]
    nblk = R // MOE_BLOCK
    return pl.pallas_call(
        _expert_body,
        grid_spec=pltpu.PrefetchScalarGridSpec(
            num_scalar_prefetch=1, grid=(nblk,),
            in_specs=[pl.BlockSpec((MOE_BLOCK, D_MODEL), lambda b, e: (b, 0)),
                      pl.BlockSpec((1, D_MODEL, EXPERT_HIDDEN), lambda b, e: (e[b], 0, 0)),
                      pl.BlockSpec((1, D_MODEL, EXPERT_HIDDEN), lambda b, e: (e[b], 0, 0)),
                      pl.BlockSpec((1, EXPERT_HIDDEN, D_MODEL), lambda b, e: (e[b], 0, 0))],
            out_specs=pl.BlockSpec((MOE_BLOCK, D_MODEL), lambda b, e: (b, 0))),
        out_shape=jax.ShapeDtypeStruct((R, D_MODEL), F32),
        compiler_params=_cparams(("arbitrary",)),
        name="moe_experts",
    )(blk_e, buf, w1, w3, w2)


def _combine_body(final, dest_ref, x_ref, wt_ref, gf_ref, yb_ref, o_ref, g_ref, sem):
    i = pl.program_id(0)

    def issue_tile(tile, slot):
        def issue(t, carry):
            for s in range(EXPERT_TOPK):
                d = dest_ref[(tile * TM + t) * EXPERT_TOPK + s]
                pltpu.make_async_copy(yb_ref.at[pl.ds(d, 1), :], g_ref.at[slot, s, pl.ds(t, 1), :],
                                      sem.at[slot]).start()
            return carry

        lax.fori_loop(0, TM, issue, 0, unroll=DMA_ISSUE_UNROLL)

    @pl.when(i == 0)
    def _():
        issue_tile(0, 0)

    @pl.when(i + 1 < pl.num_programs(0))
    def _():
        issue_tile(i + 1, (i + 1) % 2)

    slot = i % 2
    for s in range(EXPERT_TOPK):
        pltpu.make_async_copy(yb_ref.at[pl.ds(0, TM), :], g_ref.at[slot, s], sem.at[slot]).wait()
    wt = wt_ref[...]
    x = x_ref[...] + (g_ref[slot, 0] * wt[:, 0:1] + g_ref[slot, 1] * wt[:, 1:2])
    if final:
        x = _rms(x, gf_ref[...])
    o_ref[...] = x


def _moe_combine(dest_flat, x1, wts, gf, yb, final):
    T = x1.shape[0]
    nt = T // TM
    return pl.pallas_call(
        functools.partial(_combine_body, final),
        grid_spec=pltpu.PrefetchScalarGridSpec(
            num_scalar_prefetch=1, grid=(nt,),
            in_specs=[pl.BlockSpec((TM, D_MODEL), lambda i, d: (i, 0)),
                      pl.BlockSpec((TM, EXPERT_TOPK), lambda i, d: (i, 0)),
                      pl.BlockSpec((1, D_MODEL), lambda i, d: (0, 0)),
                      pl.BlockSpec(memory_space=pl.ANY)],
            out_specs=pl.BlockSpec((TM, D_MODEL), lambda i, d: (i, 0)),
            scratch_shapes=[pltpu.VMEM((2, EXPERT_TOPK, TM, D_MODEL), F32), pltpu.SemaphoreType.DMA((2,))]),
        out_shape=jax.ShapeDtypeStruct((T, D_MODEL), F32),
        compiler_params=_cparams(("arbitrary",)),
        name="moe_combine",
    )(dest_flat, x1, wts, gf, yb)


def _pad_cols(w, n):
    return jnp.pad(w, ((0, 0), (0, n - w.shape[1])))


def _rwkv_cols(w_rw, w_vm):
    W = BRANCH_WIDTH
    o1, o2, o3, o4 = 3 * W, 3 * W + DECAY_LORA, 3 * W + DECAY_LORA + ICLR_LORA, 3 * W + DECAY_LORA + ICLR_LORA + GATE_LORA
    return jnp.concatenate([w_rw[:, :o1], _pad_cols(w_rw[:, o1:o2], LANES), _pad_cols(w_rw[:, o2:o3], LANES),
                            w_rw[:, o3:o4], _pad_cols(w_vm, LANES)], axis=1)


def _pad_rows(w, n):
    return jnp.pad(w, ((0, n - w.shape[0]), (0, 0)))


def kernel(x, norm1_g, w_in, w_vmix, rwkv_mu, vmix_mu, w0, w_decay_up, a0, w_iclr_up, w_gate_up, k_k, k_a, r_k,
           lnx_g, lnx_b, v0, w_vmix_up, pool_w, pool_scale, w_branch, w_out, norm2_g, router_grp_w, router_grp_b,
           router_exp_w, router_exp_b, exp_w1, exp_w3, exp_w2, final_norm_g):
    B, S, D = x.shape
    assert D == D_MODEL and S % (TM * ATTN_GROUP) == 0
    T = B * S
    ns = S // TM
    nb = ns
    xf = x.reshape(T, D)

    slopes = jnp.asarray([2.0 ** (-8.0 * (h + 1) / ATTN_HEADS) * LOG2E for h in range(ATTN_HEADS)], F32)
    slopes = slopes.reshape(N_PAIR, 2, 1)
    s_hi = slopes.astype(BF16).astype(F32)
    s_lo = slopes - s_hi
    slope_cols = jnp.zeros((N_PAIR, 2, TM, LANES), F32)
    for col, val in enumerate((s_hi, s_lo, s_hi * TM, s_lo * TM)):
        slope_cols = slope_cols.at[:, :, :, col].set(val)
    slope_cols = slope_cols.astype(BF16)
    key_row = jnp.arange(ATTN_GROUP * TM, dtype=jnp.int32)
    in_blk, blk_no = (key_row % TM).astype(F32), (key_row // TM).astype(F32)
    ccol = jnp.zeros((ATTN_GROUP * TM, LANES), F32)
    for col, val in enumerate((in_blk, in_blk, blk_no, blk_no)):
        ccol = ccol.at[:, col].set(val)
    ccol = ccol.astype(BF16)

    n_assign = T * EXPERT_TOPK
    R = -(-(n_assign + N_EXPERTS * (MOE_BLOCK - 1)) // MOE_BLOCK) * MOE_BLOCK
    nblk = R // MOE_BLOCK

    v_first = jnp.zeros((T, BRANCH_WIDTH), F32)
    for l in range(DEPTH):
        wl = w_in[l]
        wa = wl[:, :2 * BRANCH_WIDTH].astype(BF16)
        wvt = wl[:, 2 * BRANCH_WIDTH:ATTN_COLS].T.astype(BF16)
        wg =wl[:, GATE_OFF:GATE_OFF + GATE_COLS].astype(BF16)
        wp = wl[:, POOL_OFF:POOL_OFF + BRANCH_WIDTH].astype(BF16)
        has_vmix = l > 0
        w_vm = w_vmix[l - 1] if has_vmix else jnp.zeros((D, VMIX_LORA), F32)
        mu_vm = vmix_mu[l - 1] if has_vmix else jnp.zeros((VMIX_LORA,), F32)
        wr = _rwkv_cols(wl[:, RWKV_OFF:], w_vm).astype(BF16)
        mu = _rwkv_cols(rwkv_mu[l][None, :], mu_vm[None, :])
        q, k, vt, gates, zp, zr, kmean = _inproj(xf, norm1_g[l][None, :], wa, wvt, wg, wp, wr)

        ids = _moba_select(q, kmean.reshape(B, nb, BRANCH_WIDTH), B, S)
        y_attn = _moba_attn(ids, q, k, vt, slope_cols, ccol, B, S)

        par = jnp.stack([w0[l], a0[l], k_k[l], k_a[l], r_k[l].reshape(-1),
                         v0[l - 1] if has_vmix else jnp.zeros((BRANCH_WIDTH,), F32),
                         jnp.zeros((BRANCH_WIDTH,), F32), jnp.zeros((BRANCH_WIDTH,), F32)])
        wvm_up = _pad_rows(w_vmix_up[l - 1], LANES) if has_vmix else jnp.zeros((LANES, BRANCH_WIDTH), F32)
        y_rwkv, v_cur = _rwkv(
            zr, v_first, mu, par, _pad_rows(w_decay_up[l], LANES), _pad_rows(w_iclr_up[l], LANES), w_gate_up[l],
            wvm_up, jnp.stack([lnx_g[l], lnx_b[l]]), has_vmix, B, S)
        if l == 0:
            v_first = v_cur

        y_pool = _pool(zp, pool_w[l].astype(BF16), pool_scale[l][None, :], B, S)

        rw = jnp.zeros((D, LANES), F32).at[:, :N_GROUPS].set(router_grp_w[l]).at[:, N_EXPERTS:2 * N_EXPERTS].set(router_exp_w[l])
        rw_hi = rw.astype(BF16)
        rw_lo = (rw - rw_hi.astype(F32)).astype(BF16)
        rb = jnp.zeros((1, LANES), F32).at[0, :N_GROUPS].set(router_grp_b[l]).at[0, N_EXPERTS:2 * N_EXPERTS].set(router_exp_b[l])
        x1, h2, eid, wts, cnt = _merge(xf, gates, y_attn, y_rwkv, y_pool, w_branch[l].astype(BF16),
                                       w_out[l].astype(BF16), norm2_g[l][None, :], jnp.stack([rw_hi, rw_lo]), rb)

        dest, pends = _moe_dest(eid, cnt)
        dest_flat = dest.reshape(-1)
        blk_row = jnp.arange(nblk, dtype=jnp.int32)[:, None] * MOE_BLOCK
        blk_e = jnp.minimum(jnp.sum((pends[0, :N_EXPERTS][None, :] <= blk_row).astype(jnp.int32), axis=1),
                            N_EXPERTS - 1)
        buf = _moe_scatter(dest_flat, h2, jnp.zeros((R, D), F32))
        yb = _moe_experts(blk_e, buf, exp_w1[l].astype(BF16), exp_w3[l].astype(BF16), exp_w2[l].astype(BF16))
        xf = _moe_combine(dest_flat, x1, wts, final_norm_g[None, :], yb, l == DEPTH - 1)
    return xf.reshape(B, S, D)
```

```python
import functools

import jax
import jax.numpy as jnp
from jax import lax
from jax.experimental import pallas as pl
from jax.experimental.pallas import tpu as pltpu

F32 = jnp.float32
BF16 = jnp.bfloat16
HIGHEST = lax.Precision.HIGHEST
LOG2E = 1.4426950408889634

D_MODEL = 1024
DEPTH = 2
BRANCH_WIDTH = 512
N_BRANCH = 3
ATTN_HEADS = 8
HEAD_DIM = 64
MOBA_BLOCK = 256
MOBA_TOPK = 3
DECAY_LORA = 64
ICLR_LORA = 64
GATE_LORA = 128
VMIX_LORA = 32
LNX_EPS = 64e-5
POOL_WINDOWS = (2, 4, 8, 16)
POOL_GROUP_DIM = 128
N_GROUPS = 4
EXPERTS_PER_GROUP = 8
N_EXPERTS = 32
EXPERT_TOPK = 2
EXPERT_HIDDEN = 512
RMS_EPS = 1e-6
ATTN_COLS = 3 * BRANCH_WIDTH
GATE_OFF = ATTN_COLS
GATE_COLS = N_BRANCH * D_MODEL
POOL_OFF = GATE_OFF + GATE_COLS
RWKV_OFF = POOL_OFF + BRANCH_WIDTH

LANES = 128
TM = 256
CHUNK = 64
N_CHUNK = TM // CHUNK
N_PAIR = ATTN_HEADS // 2
RW_COLS = 2048
POOL_HALO = 16
ATTN_GROUP = 4
MOE_BLOCK = 256
DMA_ISSUE_UNROLL = 8
V_ROWS = 80
VMEM_LIMIT = 56 * 1024 * 1024


def _cparams(sem):
    return pltpu.CompilerParams(dimension_semantics=sem, vmem_limit_bytes=VMEM_LIMIT)


def _const_spec(shape):
    nd = len(shape)
    return pl.BlockSpec(shape, lambda *a: (0,) * nd, pipeline_mode=pl.Buffered(1))


def _sigmoid(x):
    return 1.0 / (1.0 + jnp.exp(-x))


def _rms(x, g):
    return x * lax.rsqrt(jnp.mean(x * x, axis=-1, keepdims=True) + RMS_EPS) * g


def _split3(x):
    h1 = x.astype(BF16)
    r1 = x - h1.astype(F32)
    h2 = r1.astype(BF16)
    h3 = (r1 - h2.astype(F32)).astype(BF16)
    return h1, h2, h3


def _dot(a, b):
    return jnp.dot(a, b, preferred_element_type=F32)


def _dot_nt(a, b):
    return lax.dot_general(a, b, (((1,), (1,)), ((), ())), preferred_element_type=F32)


def _dot_tn(a, b):
    return lax.dot_general(a, b, (((0,), (0,)), ((), ())), preferred_element_type=F32)


def _inproj_body(x_ref, g_ref, wa_ref, wvt_ref, wg_ref, wp_ref, wr_ref,
                 q_ref, k_ref, vt_ref, gate_ref, zp_ref, zr_ref, km_ref):
    hb = _rms(x_ref[...], g_ref[...]).astype(BF16)
    za = _dot(hb, wa_ref[...])
    q_ref[...] = (za[:, :BRANCH_WIDTH] * (HEAD_DIM ** -0.5 * LOG2E)).astype(BF16)
    k = za[:, BRANCH_WIDTH:]
    k_ref[...] = k.astype(BF16)
    vt = _dot_nt(wvt_ref[...], hb)
    ones = jnp.ones((V_ROWS - HEAD_DIM, TM), BF16)
    for hd in range(ATTN_HEADS):
        vt_ref[0, hd // 2, hd % 2, 0:HEAD_DIM, :] = vt[hd * HEAD_DIM:(hd + 1) * HEAD_DIM].astype(BF16)
        vt_ref[0, hd // 2, hd % 2, HEAD_DIM:, :] = ones
    km_ref[0] = jnp.mean(k, axis=0, keepdims=True)
    gate_ref[...] = _sigmoid(_dot(hb, wg_ref[...])).astype(BF16)
    zp_ref[...] = _dot(hb, wp_ref[...])
    zr_ref[...] = _dot(hb, wr_ref[...])


def _inproj(xf, g, wa, wvt, wg, wp, wr):
    T = xf.shape[0]
    nt = T // TM
    row = lambda w: pl.BlockSpec((TM, w), lambda i: (i, 0))
    return pl.pallas_call(
        _inproj_body,
        grid=(nt,),
        in_specs=[row(D_MODEL), _const_spec((1, D_MODEL)), _const_spec(wa.shape), _const_spec(wvt.shape),
                  _const_spec(wg.shape), _const_spec(wp.shape), _const_spec(wr.shape)],
        out_specs=[row(BRANCH_WIDTH), row(BRANCH_WIDTH),
                   pl.BlockSpec((1, N_PAIR, 2, V_ROWS, TM), lambda i: (i, 0, 0, 0, 0)),
                   row(GATE_COLS), row(BRANCH_WIDTH), row(RW_COLS),
                   pl.BlockSpec((1, 1, BRANCH_WIDTH), lambda i: (i, 0, 0))],
        out_shape=[jax.ShapeDtypeStruct((T, BRANCH_WIDTH), BF16), jax.ShapeDtypeStruct((T, BRANCH_WIDTH), BF16),
                   jax.ShapeDtypeStruct((nt, N_PAIR, 2, V_ROWS, TM), BF16),
                   jax.ShapeDtypeStruct((T, GATE_COLS), BF16), jax.ShapeDtypeStruct((T, BRANCH_WIDTH), F32),
                   jax.ShapeDtypeStruct((T, RW_COLS), F32), jax.ShapeDtypeStruct((nt, 1, BRANCH_WIDTH), F32)],
        compiler_params=_cparams(("parallel",)),
        name="inproj",
    )(xf, g, wa, wvt, wg, wp, wr)


def _select_body(q_ref, km_ref, o_ref):
    i = pl.program_id(1)
    nb = km_ref.shape[1]
    lane = lax.broadcasted_iota(jnp.int32, (1, LANES), 1)
    blk = lax.broadcasted_iota(jnp.int32, (nb, TM), 0)
    for p in range(N_PAIR):
        q = q_ref[:, p * LANES:(p + 1) * LANES]
        km = km_ref[0, :, p * LANES:(p + 1) * LANES]
        rows = []
        for half in range(2):
            in_head = (lane < HEAD_DIM) if half == 0 else (lane >= HEAD_DIM)
            g = sum(_dot_nt(part, q) for part in _split3(jnp.where(in_head, km, 0.0)))
            g = jnp.where(blk < i, g, -jnp.inf)
            for _ in range(MOBA_TOPK):
                mx = jnp.max(g, axis=0, keepdims=True)
                idx = jnp.min(jnp.where(g == mx, blk, nb), axis=0, keepdims=True)
                rows.append(jnp.where(mx > -jnp.inf, idx, -1))
                g = jnp.where(blk == idx, -jnp.inf, g)
        rows += [jnp.full((1, TM), -1, jnp.int32)] * 2
        o_ref[p] = jnp.concatenate(rows, axis=0)


def _moba_select(q, kmean, B, S):
    ns = S // TM
    nb = kmean.shape[1]
    return pl.pallas_call(
        _select_body,
        grid=(B, ns),
        in_specs=[pl.BlockSpec((TM, BRANCH_WIDTH), lambda b, i: (b * ns + i, 0)),
                  pl.BlockSpec((1, nb, BRANCH_WIDTH), lambda b, i: (b, 0, 0))],
        out_specs=pl.BlockSpec((N_PAIR, 8, TM), lambda b, i: (b, 0, i)),
        out_shape=jax.ShapeDtypeStruct((B * N_PAIR, 8, S), jnp.int32),
        compiler_params=_cparams(("parallel", "parallel")),
        name="moba_select",
    )(q, kmean)


def _attn_body(ids_ref, q_ref, k_ref, vt_ref, slope_ref, ccol_ref, o_ref, qa_ref, m_ref, acc_ref, sa_ref, sb_ref,
               mxa_ref, mxb_ref, pa_ref, pb_ref):
    i = pl.program_id(2)
    lane = lax.broadcasted_iota(jnp.int32, (1, LANES), 1)
    first = lane < HEAD_DIM
    q = q_ref[...]
    zero = jnp.zeros_like(q)
    qa_ref[0, :, 0:LANES] = jnp.where(first, q, zero)
    qa_ref[1, :, 0:LANES] = jnp.where(first, zero, q)
    qa_ref[:, :, LANES:] = slope_ref[0]
    ids = ids_ref[0]

    def scores(j0, nblk):
        kj = k_ref[pl.ds(pl.multiple_of(j0 * TM, TM), nblk * TM), :]
        kaug = jnp.concatenate([kj, ccol_ref[0:nblk * TM, :]], axis=1)
        return [_dot_nt(kaug, qa_ref[h]) for h in range(2)]

    def pv(h, p, j0, nblk):
        vaug = jnp.concatenate([vt_ref[j0 + u, 0, h] for u in range(nblk)], axis=1)
        return _dot(vaug, p)

    kidx = lax.broadcasted_iota(jnp.int32, (TM, TM), 0)
    qidx = lax.broadcasted_iota(jnp.int32, (TM, TM), 1)
    def own_block():
        for h, s in enumerate(scores(i, 1)):
            s = jnp.where(kidx <= qidx, s, -jnp.inf)
            m0 = jnp.max(s, axis=0, keepdims=True)
            m_ref[h] = m0
            acc_ref[h] = pv(h, jnp.exp2(s - m0).astype(BF16), i, 1)

    def score_block(g, u, buf, mx):
        kj = k_ref[pl.ds(pl.multiple_of((g * ATTN_GROUP + u) * TM, TM), TM), :]
        kaug = jnp.concatenate([kj, ccol_ref[u * TM:(u + 1) * TM, :]], axis=1)
        for h in range(2):
            s = _dot_nt(kaug, qa_ref[h])
            buf[h, u * TM:(u + 1) * TM, :] = s
            parts = [s[r:r + 8] for r in range(0, 64, 8)]
            for r in range(64, TM, 8):
                parts[(r // 8) % 8] = jnp.maximum(parts[(r // 8) % 8], s[r:r + 8])
            while len(parts) > 1:
                parts = [jnp.maximum(parts[n], parts[n + 1]) for n in range(0, len(parts), 2)]
            mx[h, u] = parts[0]

    def stage(g, cur, nxt, g_next):
        cur_s, cur_mx, cur_p = cur
        j0 = g * ATTN_GROUP
        dist = ((j0 - i) * TM).astype(F32)
        stats = []
        for h in range(2):
            slope = slope_ref[0, h, 0:1, 0:1].astype(F32) + slope_ref[0, h, 0:1, 1:2].astype(F32)
            c0 = slope * dist
            r0 = h * MOBA_TOPK
            sels, best = [], None
            for u in range(ATTN_GROUP):
                j = j0 + u
                sel = (ids[r0:r0 + 1] == j) | (ids[r0 + 1:r0 + 2] == j) | (ids[r0 + 2:r0 + 3] == j)
                cand = jnp.where(sel, cur_mx[h, u], -jnp.inf)
                best = cand if best is None else jnp.maximum(best, cand)
                sels.append(sel)
            m_old = m_ref[h]
            m_new = jnp.maximum(m_old, jnp.max(best, axis=0, keepdims=True) + c0)
            m_ref[h] = m_new
            stats.append((jnp.exp2(m_old - m_new), [jnp.where(sel, m_new - c0, jnp.inf) for sel in sels]))
        for u in range(ATTN_GROUP):
            if nxt is not None:
                score_block(g_next, u, nxt[0], nxt[1])
            for h in range(2):
                cur_p[h, u * TM:(u + 1) * TM, :] = jnp.exp2(cur_s[h, u * TM:(u + 1) * TM, :] - stats[h][1][u]).astype(BF16)
        for h in range(2):
            acc_ref[h] = acc_ref[h] * stats[h][0] + pv(h, cur_p[h], j0, ATTN_GROUP)

    n_group = (i + ATTN_GROUP - 1) // ATTN_GROUP
    buf_a, buf_b = (sa_ref, mxa_ref, pa_ref), (sb_ref, mxb_ref, pb_ref)

    for u in range(ATTN_GROUP):
        score_block(0, u, sa_ref, mxa_ref)
    own_block()

    def pair_body(t, carry):
        stage(2 * t, buf_a, buf_b, 2 * t + 1)
        stage(2 * t + 1, buf_b, buf_a, jnp.minimum(2 * t + 2, n_group - 1))
        return carry

    lax.fori_loop(0, n_group // 2, pair_body, 0)

    @pl.when(n_group % 2 == 1)
    def _():
        stage(n_group - 1, buf_a, None, None)

    acc_a, acc_b = acc_ref[0], acc_ref[1]
    out_t = jnp.concatenate([acc[:HEAD_DIM] / acc[HEAD_DIM:HEAD_DIM + 1] for acc in (acc_a, acc_b)], axis=0)
    o_ref[...] = out_t.T.astype(BF16)


def _moba_attn(ids, q, k, vt, slope_cols, ccol, B, S):
    ns = S // TM
    return pl.pallas_call(
        _attn_body,
        grid=(B, N_PAIR, ns),
        in_specs=[pl.BlockSpec((1, 8, TM), lambda b, p, i: (b * N_PAIR + p, 0, i)),
                  pl.BlockSpec((TM, LANES), lambda b, p, i: (b * ns + i, p)),
                  pl.BlockSpec((S, LANES), lambda b, p, i: (b, p)),
                  pl.BlockSpec((ns, 1, 2, V_ROWS, TM), lambda b, p, i: (b, p, 0, 0, 0)),
                  pl.BlockSpec((1, 2, TM, LANES), lambda b, p, i: (p, 0, 0, 0)),
                  pl.BlockSpec((ATTN_GROUP * TM, LANES), lambda b, p, i: (0, 0))],
        out_specs=pl.BlockSpec((TM, LANES), lambda b, p, i: (b * ns + i, p)),
        out_shape=jax.ShapeDtypeStruct((B * S, BRANCH_WIDTH), BF16),
        scratch_shapes=[pltpu.VMEM((2, TM, 2 * LANES), BF16), pltpu.VMEM((2, 1, TM), F32),
                        pltpu.VMEM((2, V_ROWS, TM), F32),
                        pltpu.VMEM((2, ATTN_GROUP * TM, TM), F32), pltpu.VMEM((2, ATTN_GROUP * TM, TM), F32),
                        pltpu.VMEM((2, ATTN_GROUP, 8, TM), F32), pltpu.VMEM((2, ATTN_GROUP, 8, TM), F32),
                        pltpu.VMEM((2, ATTN_GROUP * TM, TM), BF16), pltpu.VMEM((2, ATTN_GROUP * TM, TM), BF16)],
        compiler_params=_cparams(("parallel", "parallel", "arbitrary")),
        name="moba_attn",
    )(ids, q, k, vt, slope_cols, ccol)


def _pool_body(z_ref, halo_ref, w_ref, sc_ref, o_ref, ext_ref):
    i = pl.program_id(1)
    ext_ref[0:POOL_HALO, :] = jnp.where(i == 0, 0.0, halo_ref[...])
    ext_ref[POOL_HALO:, :] = z_ref[...]
    t1 = (i * TM + 1 + lax.broadcasted_iota(jnp.int32, (TM, 1), 0)).astype(F32)
    for gi, win in enumerate(POOL_WINDOWS):
        cols = slice(gi * POOL_GROUP_DIM, (gi + 1) * POOL_GROUP_DIM)
        acc = ext_ref[POOL_HALO:, cols]
        for d in range(1, win):
            acc = acc + ext_ref[POOL_HALO - d:POOL_HALO - d + TM, cols]
        p = acc / jnp.minimum(t1, float(win)) - ext_ref[POOL_HALO:, cols]
        y = _dot(p.astype(BF16), w_ref[gi])
        o_ref[:, cols] = (y * sc_ref[:, cols]).astype(BF16)


def _pool(zp, w, sc, B, S):
    ns = S // TM
    hb = TM // POOL_HALO
    return pl.pallas_call(
        _pool_body,
        grid=(B, ns),
        in_specs=[pl.BlockSpec((TM, BRANCH_WIDTH), lambda b, i: (b * ns + i, 0)),
                  pl.BlockSpec((POOL_HALO, BRANCH_WIDTH), lambda b, i: (jnp.maximum((b * ns + i) * hb - 1, 0), 0)),
                  _const_spec(w.shape), _const_spec(sc.shape)],
        out_specs=pl.BlockSpec((TM, BRANCH_WIDTH), lambda b, i: (b * ns + i, 0)),
        out_shape=jax.ShapeDtypeStruct((B * S, BRANCH_WIDTH), BF16),
        scratch_shapes=[pltpu.VMEM((POOL_HALO + TM, BRANCH_WIDTH), F32)],
        compiler_params=_cparams(("parallel", "parallel")),
        name="pool",
    )(zp, zp, w, sc)


def _seg_sum(x, seg):
    hi = x.astype(BF16)
    lo = (x - hi.astype(F32)).astype(BF16)
    outs = []
    for p in range(N_PAIR):
        cols = slice(p * LANES, (p + 1) * LANES)
        outs.append(_dot(hi[:, cols], seg) + _dot(lo[:, cols], seg))
    return jnp.concatenate(outs, axis=1)


def _rwkv_body(has_vmix, z_ref, halo_ref, vf_ref, mu_ref, par_ref, wd_ref, wa_ref, wg_ref, wvm_ref, ln_ref,
               o_ref, v_ref, s_ref):
    i = pl.program_id(1)

    @pl.when(i == 0)
    def _():
        s_ref[...] = jnp.zeros_like(s_ref)

    W = BRANCH_WIDTH
    z = z_ref[...]
    prev0 = jnp.where(i == 0, 0.0, halo_ref[7:8, :])
    row = lax.broadcasted_iota(jnp.int32, (TM, 1), 0)
    prev = jnp.where(row == 0, prev0, pltpu.roll(z, 1, axis=0))
    rw = z + (prev - z) * mu_ref[...]
    r, k, v = rw[:, 0:W], rw[:, W:2 * W], rw[:, 2 * W:3 * W]
    w_lo = rw[:, 3 * W:3 * W + LANES]
    a_lo = rw[:, 3 * W + LANES:3 * W + 2 * LANES]
    g_lo = rw[:, 3 * W + 2 * LANES:3 * W + 3 * LANES]
    vm_lo = rw[:, 3 * W + 3 * LANES:3 * W + 4 * LANES]
    w0, a0, k_k, k_a, r_k, v0 = (par_ref[n:n + 1, :] for n in range(6))
    hdot = functools.partial(jnp.dot, precision=HIGHEST, preferred_element_type=F32)

    u = w0 + hdot(jnp.tanh(w_lo), wd_ref[...])
    nu = -u
    softplus = jnp.maximum(nu, 0.0) + jnp.log(1.0 + jnp.exp(-jnp.abs(nu)))
    logw = -jnp.exp(-softplus - 0.5)
    if has_vmix:
        v = v + (vf_ref[...] - v) * _sigmoid(v0 + hdot(vm_lo, wvm_ref[...]))
    v_ref[...] = v
    a = _sigmoid(a0 + hdot(a_lo, wa_ref[...]))
    gate = hdot(_sigmoid(g_lo), wg_ref[...])

    lane = lax.broadcasted_iota(jnp.int32, (1, LANES), 1)
    first = lane < HEAD_DIM
    seg = ((lax.broadcasted_iota(jnp.int32, (LANES, LANES), 0) < HEAD_DIM)
           == (lax.broadcasted_iota(jnp.int32, (LANES, LANES), 1) < HEAD_DIM)).astype(BF16)
    kk = k * k_k
    kk = kk / jnp.maximum(jnp.sqrt(_seg_sum(kk * kk, seg)), 1e-12)
    k2 = k * (1.0 + (a - 1.0) * k_a)
    bonus = _seg_sum(r * k2 * r_k, seg) * v

    ri = lax.broadcasted_iota(jnp.int32, (TM, TM), 0)
    ci = lax.broadcasted_iota(jnp.int32, (TM, TM), 1)
    same = (ri // CHUNK) == (ci // CHUNK)
    incl = same & (ri >= ci)
    strict = same & (ri > ci)
    lmat = incl.astype(BF16)
    h1, h2, h3 = _split3(logw)
    cw = _dot(lmat, h1) + _dot(lmat, h2) + _dot(lmat, h3)
    e_neg = jnp.exp(-cw)
    at = (-kk) * jnp.exp(cw - logw)
    bt = (kk * a) * e_neg
    kt = k2 * e_neg
    rt = r * jnp.exp(cw)
    eye_pk = (lax.broadcasted_iota(jnp.int32, (CHUNK, TM), 0)
              == lax.broadcasted_iota(jnp.int32, (CHUNK, TM), 1) % CHUNK).astype(F32)
    bd = ((lax.broadcasted_iota(jnp.int32, (LANES, LANES), 0) < HEAD_DIM)
          == (lax.broadcasted_iota(jnp.int32, (LANES, LANES), 1) < HEAD_DIM))
    eye_l = (lax.broadcasted_iota(jnp.int32, (LANES, LANES), 0)
             == lax.broadcasted_iota(jnp.int32, (LANES, LANES), 1)).astype(F32)

    def block_diag(packed):
        return jnp.where(same, jnp.concatenate([packed] * N_CHUNK, axis=0), jnp.zeros((), packed.dtype))

    heads = []
    for hd in range(ATTN_HEADS):
        cols = slice((hd // 2) * LANES, (hd // 2 + 1) * LANES)
        mh = first if hd % 2 == 0 else jnp.logical_not(first)
        am = jnp.where(mh, at[:, cols], 0.0)
        rm = jnp.where(mh, rt[:, cols], 0.0)
        vm = jnp.where(mh, v[:, cols], 0.0).astype(BF16)
        bk = jnp.concatenate([bt[:, cols], kt[:, cols]], axis=0).astype(BF16)
        big = _dot_nt(jnp.concatenate([am, rm], axis=0).astype(BF16), bk)
        n = jnp.where(strict, big[:TM, :TM], 0.0)
        n_pk = n[0:CHUNK]
        for c in range(1, N_CHUNK):
            n_pk = n_pk + n[c * CHUNK:(c + 1) * CHUNK]
        heads.append(dict(
            am=am, rm=rm, vm=vm, t_pk=eye_pk + n_pk, p_pk=n_pk, p_bd=n.astype(BF16),
            aak=jnp.where(strict, big[:TM, TM:], 0.0).astype(BF16),
            mrb=jnp.where(incl, big[TM:, :TM], 0.0).astype(BF16),
            mrk=jnp.where(incl, big[TM:, TM:], 0.0).astype(BF16)))
    for _ in range(CHUNK.bit_length() - 2):
        for hs in heads:
            hs["p_pk"] = _dot(hs["p_pk"].astype(BF16), hs["p_bd"])
        for hs in heads:
            hs["p_bd"] = block_diag(hs["p_pk"].astype(BF16))
            hs["t_pk"] = hs["t_pk"] + _dot(hs["t_pk"].astype(BF16), hs["p_bd"])
    for hs in heads:
        hs["av"] = _dot(hs["aak"], hs["vm"])
    for hs in heads:
        x = _dot(block_diag(hs["t_pk"].astype(BF16)),
                 jnp.concatenate([hs["am"], hs["av"]], axis=1).astype(BF16))
        hs["ph"], hs["qh"] = x[:, :LANES], x[:, LANES:]
    for hs in heads:
        hs["rp"] = hs["rm"] + _dot(hs["mrb"], hs["ph"].astype(BF16))
        hs["y0"] = _dot(hs["mrb"], hs["qh"].astype(BF16)) + _dot(hs["mrk"], hs["vm"])

    pairs = []
    for p in range(N_PAIR):
        cols = slice(p * LANES, (p + 1) * LANES)
        ha, hb = heads[2 * p], heads[2 * p + 1]
        bp, kp, vp = bt[:, cols], kt[:, cols], v[:, cols]
        p_pair = ha["ph"] + hb["ph"]
        q_pair = ha["qh"] + hb["qh"]
        gmats, hmats = [], []
        for c in range(N_CHUNK):
            rows = slice(c * CHUNK, (c + 1) * CHUNK)
            wc = jnp.exp(cw[(c + 1) * CHUNK - 1:(c + 1) * CHUNK, cols])
            bc = bp[rows].astype(BF16)
            gp = _dot_tn(p_pair[rows].astype(BF16), bc)
            hp = _dot_tn(jnp.concatenate([q_pair[rows], vp[rows]], axis=0).astype(BF16),
                         jnp.concatenate([bp[rows], kp[rows]], axis=0).astype(BF16))
            gmats.append(((eye_l + jnp.where(bd, gp, 0.0)) * wc).astype(BF16))
            hmats.append(jnp.where(bd, hp, 0.0) * wc)
        pairs.append(dict(rp=(ha["rp"] + hb["rp"]).astype(BF16), y0=ha["y0"] + hb["y0"], g=gmats, h=hmats))

    states = [s_ref[p] for p in range(N_PAIR)]
    yc = [[] for _ in range(N_PAIR)]
    for c in range(N_CHUNK):
        rows = slice(c * CHUNK, (c + 1) * CHUNK)
        for p, pr in enumerate(pairs):
            s16 = states[p].astype(BF16)
            yc[p].append(_dot_nt(pr["rp"][rows], s16) + pr["y0"][rows])
            states[p] = _dot(s16, pr["g"][c]) + pr["h"][c]
    for p in range(N_PAIR):
        s_ref[p] = states[p]
    y = jnp.concatenate([jnp.concatenate(yc[p], axis=0) for p in range(N_PAIR)], axis=1)
    mean = _seg_sum(y, seg) * (1.0 / HEAD_DIM)
    d = y - mean
    var = _seg_sum(d * d, seg) * (1.0 / HEAD_DIM)
    yn = d * lax.rsqrt(var + LNX_EPS) * ln_ref[0:1, :] + ln_ref[1:2, :]
    o_ref[...] = ((yn + bonus) * gate).astype(BF16)


def _rwkv(zr, vfirst, mu, par, wd, wa, wg, wvm, ln, has_vmix, B, S):
    ns = S // TM
    T = B * S
    tile = lambda w: pl.BlockSpec((TM, w), lambda b, i: (b * ns + i, 0))
    return pl.pallas_call(
        functools.partial(_rwkv_body, has_vmix),
        grid=(B, ns),
        in_specs=[tile(RW_COLS),
                  pl.BlockSpec((8, RW_COLS), lambda b, i: (jnp.maximum((b * ns + i) * (TM // 8) - 1, 0), 0)),
                  tile(BRANCH_WIDTH), _const_spec(mu.shape), _const_spec(par.shape), _const_spec(wd.shape),
                  _const_spec(wa.shape), _const_spec(wg.shape), _const_spec(wvm.shape), _const_spec(ln.shape)],
        out_specs=[tile(BRANCH_WIDTH), tile(BRANCH_WIDTH)],
        out_shape=[jax.ShapeDtypeStruct((T, BRANCH_WIDTH), BF16), jax.ShapeDtypeStruct((T, BRANCH_WIDTH), F32)],
        scratch_shapes=[pltpu.VMEM((N_PAIR, LANES, LANES), F32)],
        compiler_params=_cparams(("parallel", "arbitrary")),
        name="rwkv",
    )(zr, zr, vfirst, mu, par, wd, wa, wg, wvm, ln)


def _merge_body(x_ref, gate_ref, ya_ref, yr_ref, yp_ref, wb_ref, wo_ref, g2_ref, rw_ref, rb_ref,
                x1_ref, h2_ref, eid_ref, wt_ref, cnt_ref):
    merged = jnp.zeros((TM, D_MODEL), F32)
    for bi, y_ref in enumerate((ya_ref, yr_ref, yp_ref)):
        merged = merged + gate_ref[:, bi * D_MODEL:(bi + 1) * D_MODEL].astype(F32) * _dot(y_ref[...], wb_ref[bi])
    x1 = x_ref[...] + _dot(merged.astype(BF16), wo_ref[...])
    x1_ref[...] = x1
    h2 = _rms(x1, g2_ref[...])
    h2_ref[...] = h2
    hi = h2.astype(BF16)
    lo = (h2 - hi.astype(F32)).astype(BF16)
    logits = _dot(hi, rw_ref[0]) + _dot(lo, rw_ref[0]) + _dot(hi, rw_ref[1]) + rb_ref[...]
    lane = lax.broadcasted_iota(jnp.int32, (TM, LANES), 1)
    ninf = -jnp.inf
    glog = jnp.where(lane < N_GROUPS, logits, ninf)
    gmax = jnp.max(glog, axis=-1, keepdims=True)
    grp = jnp.min(jnp.where(glog == gmax, lane, LANES), axis=-1, keepdims=True)
    gprob = 1.0 / jnp.sum(jnp.exp(glog - gmax), axis=-1, keepdims=True)
    e_lane = lane - N_EXPERTS
    in_grp = (e_lane >= grp * EXPERTS_PER_GROUP) & (e_lane < (grp + 1) * EXPERTS_PER_GROUP)
    elog = jnp.where(in_grp, logits, ninf)
    v1 = jnp.max(elog, axis=-1, keepdims=True)
    i1 = jnp.min(jnp.where(elog == v1, e_lane, LANES), axis=-1, keepdims=True)
    elog = jnp.where(e_lane == i1, ninf, elog)
    v2 = jnp.max(elog, axis=-1, keepdims=True)
    i2 = jnp.min(jnp.where(elog == v2, e_lane, LANES), axis=-1, keepdims=True)
    e2 = jnp.exp(v2 - v1)
    den = 1.0 + e2
    eid_ref[...] = jnp.concatenate([i1, i2], axis=1)
    wt_ref[...] = jnp.concatenate([gprob / den, gprob * e2 / den], axis=1)
    hit = ((lane == i1) | (lane == i2)).astype(F32)
    cnt_ref[0] = jnp.sum(hit, axis=0, keepdims=True).astype(jnp.int32)


def _merge(xf, gates, ya, yr, yp, wb, wo, g2, rw, rb):
    T = xf.shape[0]
    nt = T // TM
    row = lambda w: pl.BlockSpec((TM, w), lambda i: (i, 0))
    return pl.pallas_call(
        _merge_body,
        grid=(nt,),
        in_specs=[row(D_MODEL), row(GATE_COLS), row(BRANCH_WIDTH), row(BRANCH_WIDTH), row(BRANCH_WIDTH),
                  _const_spec(wb.shape), _const_spec(wo.shape), _const_spec(g2.shape), _const_spec(rw.shape),
                  _const_spec(rb.shape)],
        out_specs=[row(D_MODEL), row(D_MODEL), row(EXPERT_TOPK), row(EXPERT_TOPK),
                   pl.BlockSpec((1, 1, LANES), lambda i: (i, 0, 0))],
        out_shape=[jax.ShapeDtypeStruct((T, D_MODEL), F32), jax.ShapeDtypeStruct((T, D_MODEL), F32),
                   jax.ShapeDtypeStruct((T, EXPERT_TOPK), jnp.int32), jax.ShapeDtypeStruct((T, EXPERT_TOPK), F32),
                   jax.ShapeDtypeStruct((nt, 1, LANES), jnp.int32)],
        compiler_params=_cparams(("parallel",)),
        name="merge",
    )(xf, gates, ya, yr, yp, wb, wo, g2, rw, rb)


def _lane_excl_cumsum(x):
    lane = lax.broadcasted_iota(jnp.int32, x.shape, 1)
    inc = x
    sh = 1
    while sh < LANES:
        inc = inc + jnp.where(lane >= sh, pltpu.roll(inc, sh, axis=1), 0)
        sh *= 2
    return inc - x


def _dest_body(eid_ref, cnt_ref, dest_ref, pend_ref, base_ref):
    i = pl.program_id(0)

    @pl.when(i == 0)
    def _():
        total = jnp.sum(cnt_ref[...], axis=0)
        shift = MOE_BLOCK.bit_length() - 1
        padded = jnp.left_shift(jnp.right_shift(total + (MOE_BLOCK - 1), shift), shift)
        pstart = _lane_excl_cumsum(padded)
        base_ref[...] = pstart
        pend_ref[...] = pstart + padded

    lane = lax.broadcasted_iota(jnp.int32, (TM, LANES), 1)
    e0, e1 = eid_ref[:, 0:1], eid_ref[:, 1:2]
    oh0, oh1 = lane == e0, lane == e1
    both = (oh0 | oh1).astype(BF16)
    lower = (lax.broadcasted_iota(jnp.int32, (TM, TM), 0)
             > lax.broadcasted_iota(jnp.int32, (TM, TM), 1)).astype(BF16)
    pos = _dot(lower, both) + base_ref[...].astype(F32)
    d0 = jnp.sum(jnp.where(oh0, pos, 0.0), axis=-1, keepdims=True)
    d1 = jnp.sum(jnp.where(oh1, pos, 0.0), axis=-1, keepdims=True)
    dest_ref[...] = jnp.concatenate([d0, d1], axis=1).astype(jnp.int32)
    base_ref[...] = base_ref[...] + jnp.sum(both.astype(F32), axis=0, keepdims=True).astype(jnp.int32)


def _moe_dest(eid, cnt):
    T = eid.shape[0]
    nt = T // TM
    return pl.pallas_call(
        _dest_body,
        grid=(nt,),
        in_specs=[pl.BlockSpec((TM, EXPERT_TOPK), lambda i: (i, 0)),
                  pl.BlockSpec((nt, 1, LANES), lambda i: (0, 0, 0))],
        out_specs=[pl.BlockSpec((TM, EXPERT_TOPK), lambda i: (i, 0)),
                   pl.BlockSpec((1, LANES), lambda i: (0, 0))],
        out_shape=[jax.ShapeDtypeStruct((T, EXPERT_TOPK), jnp.int32), jax.ShapeDtypeStruct((1, LANES), jnp.int32)],
        scratch_shapes=[pltpu.VMEM((1, LANES), jnp.int32)],
        compiler_params=_cparams(("arbitrary",)),
        name="moe_dest",
    )(eid, cnt)


def _scatter_body(dest_ref, h_ref, buf_in_ref, buf_ref, sem):
    del buf_in_ref
    i = pl.program_id(0)

    def issue(t, carry):
        for s in range(EXPERT_TOPK):
            d = dest_ref[(i * TM + t) * EXPERT_TOPK + s]
            pltpu.make_async_copy(h_ref.at[pl.ds(t, 1), :], buf_ref.at[pl.ds(d, 1), :], sem).start()
        return carry

    lax.fori_loop(0, TM, issue, 0, unroll=DMA_ISSUE_UNROLL)
    for _ in range(EXPERT_TOPK):
        pltpu.make_async_copy(h_ref, buf_ref.at[pl.ds(0, TM), :], sem).wait()


def _moe_scatter(dest_flat, h2, buf0):
    T = h2.shape[0]
    nt = T // TM
    return pl.pallas_call(
        _scatter_body,
        grid_spec=pltpu.PrefetchScalarGridSpec(
            num_scalar_prefetch=1, grid=(nt,),
            in_specs=[pl.BlockSpec((TM, D_MODEL), lambda i, d: (i, 0)), pl.BlockSpec(memory_space=pl.ANY)],
            out_specs=pl.BlockSpec(memory_space=pl.ANY),
            scratch_shapes=[pltpu.SemaphoreType.DMA(())]),
        out_shape=jax.ShapeDtypeStruct(buf0.shape, F32),
        input_output_aliases={2: 0},
        compiler_params=_cparams(("arbitrary",)),
        name="moe_scatter",
    )(dest_flat, h2, buf0)


def _expert_body(be_ref, x_ref, w1_ref, w3_ref, w2_ref, o_ref):
    del be_ref
    xb = x_ref[...].astype(BF16)
    h1 = _dot(xb, w1_ref[0])
    h3 = _dot(xb, w3_ref[0])
    hid = (h1 * _sigmoid(h1)) * h3
    o_ref[...] = _dot(hid.astype(BF16), w2_ref[0])


def _moe_experts(blk_e, buf, w1, w3, w2):
    R = buf.shape[0]
    nblk = R // MOE_BLOCK
    return pl.pallas_call(
        _expert_body,
        grid_spec=pltpu.PrefetchScalarGridSpec(
            num_scalar_prefetch=1, grid=(nblk,),
            in_specs=[pl.BlockSpec((MOE_BLOCK, D_MODEL), lambda b, e: (b, 0)),
                      pl.BlockSpec((1, D_MODEL, EXPERT_HIDDEN), lambda b, e: (e[b], 0, 0)),
                      pl.BlockSpec((1, D_MODEL, EXPERT_HIDDEN), lambda b, e: (e[b], 0, 0)),
                      pl.BlockSpec((1, EXPERT_HIDDEN, D_MODEL), lambda b, e: (e[b], 0, 0))],
            out_specs=pl.BlockSpec((MOE_BLOCK, D_MODEL), lambda b, e: (b, 0))),
        out_shape=jax.ShapeDtypeStruct((R, D_MODEL), F32),
        compiler_params=_cparams(("arbitrary",)),
        name="moe_experts",
    )(blk_e, buf, w1, w3, w2)


def _combine_body(final, dest_ref, x_ref, wt_ref, gf_ref, yb_ref, o_ref, g_ref, sem):
    i = pl.program_id(0)

    def issue_tile(tile, slot):
        def issue(t, carry):
            for s in range(EXPERT_TOPK):
                d = dest_ref[(tile * TM + t) * EXPERT_TOPK + s]
                pltpu.make_async_copy(yb_ref.at[pl.ds(d, 1), :], g_ref.at[slot, s, pl.ds(t, 1), :],
                                      sem.at[slot]).start()
            return carry

        lax.fori_loop(0, TM, issue, 0, unroll=DMA_ISSUE_UNROLL)

    @pl.when(i == 0)
    def _():
        issue_tile(0, 0)

    @pl.when(i + 1 < pl.num_programs(0))
    def _():
        issue_tile(i + 1, (i + 1) % 2)

    slot = i % 2
    for s in range(EXPERT_TOPK):
        pltpu.make_async_copy(yb_ref.at[pl.ds(0, TM), :], g_ref.at[slot, s], sem.at[slot]).wait()
    wt = wt_ref[...]
    x = x_ref[...] + (g_ref[slot, 0] * wt[:, 0:1] + g_ref[slot, 1] * wt[:, 1:2])
    if final:
        x = _rms(x, gf_ref[...])
    o_ref[...] = x


def _moe_combine(dest_flat, x1, wts, gf, yb, final):
    T = x1.shape[0]
    nt = T // TM
    return pl.pallas_call(
        functools.partial(_combine_body, final),
        grid_spec=pltpu.PrefetchScalarGridSpec(
            num_scalar_prefetch=1, grid=(nt,),
            in_specs=[pl.BlockSpec((TM, D_MODEL), lambda i, d: (i, 0)),
                      pl.BlockSpec((TM, EXPERT_TOPK), lambda i, d: (i, 0)),
                      pl.BlockSpec((1, D_MODEL), lambda i, d: (0, 0)),
                      pl.BlockSpec(memory_space=pl.ANY)],
            out_specs=pl.BlockSpec((TM, D_MODEL), lambda i, d: (i, 0)),
            scratch_shapes=[pltpu.VMEM((2, EXPERT_TOPK, TM, D_MODEL), F32), pltpu.SemaphoreType.DMA((2,))]),
        out_shape=jax.ShapeDtypeStruct((T, D_MODEL), F32),
        compiler_params=_cparams(("arbitrary",)),
        name="moe_combine",
    )(dest_flat, x1, wts, gf, yb)


def _pad_cols(w, n):
    return jnp.pad(w, ((0, 0), (0, n - w.shape[1])))


def _rwkv_cols(w_rw, w_vm):
    W = BRANCH_WIDTH
    o1, o2, o3, o4 = 3 * W, 3 * W + DECAY_LORA, 3 * W + DECAY_LORA + ICLR_LORA, 3 * W + DECAY_LORA + ICLR_LORA + GATE_LORA
    return jnp.concatenate([w_rw[:, :o1], _pad_cols(w_rw[:, o1:o2], LANES), _pad_cols(w_rw[:, o2:o3], LANES),
                            w_rw[:, o3:o4], _pad_cols(w_vm, LANES)], axis=1)


def _pad_rows(w, n):
    return jnp.pad(w, ((0, n - w.shape[0]), (0, 0)))


def kernel(x, norm1_g, w_in, w_vmix, rwkv_mu, vmix_mu, w0, w_decay_up, a0, w_iclr_up, w_gate_up, k_k, k_a, r_k,
           lnx_g, lnx_b, v0, w_vmix_up, pool_w, pool_scale, w_branch, w_out, norm2_g, router_grp_w, router_grp_b,
           router_exp_w, router_exp_b, exp_w1, exp_w3, exp_w2, final_norm_g):
    B, S, D = x.shape
    assert D == D_MODEL and S % (TM * ATTN_GROUP) == 0
    T = B * S
    ns = S // TM
    nb = ns
    xf = x.reshape(T, D)

    slopes = jnp.asarray([2.0 ** (-8.0 * (h + 1) / ATTN_HEADS) * LOG2E for h in range(ATTN_HEADS)], F32)
    slopes = slopes.reshape(N_PAIR, 2, 1)
    s_hi = slopes.astype(BF16).astype(F32)
    s_lo = slopes - s_hi
    slope_cols = jnp.zeros((N_PAIR, 2, TM, LANES), F32)
    for col, val in enumerate((s_hi, s_lo, s_hi * TM, s_lo * TM)):
        slope_cols = slope_cols.at[:, :, :, col].set(val)
    slope_cols = slope_cols.astype(BF16)
    key_row = jnp.arange(ATTN_GROUP * TM, dtype=jnp.int32)
    in_blk, blk_no = (key_row % TM).astype(F32), (key_row // TM).astype(F32)
    ccol = jnp.zeros((ATTN_GROUP * TM, LANES), F32)
    for col, val in enumerate((in_blk, in_blk, blk_no, blk_no)):
        ccol = ccol.at[:, col].set(val)
    ccol = ccol.astype(BF16)

    n_assign = T * EXPERT_TOPK
    R = -(-(n_assign + N_EXPERTS * (MOE_BLOCK - 1)) // MOE_BLOCK) * MOE_BLOCK
    nblk = R // MOE_BLOCK

    v_first = jnp.zeros((T, BRANCH_WIDTH), F32)
    for l in range(DEPTH):
        wl = w_in[l]
        wa = wl[:, :2 * BRANCH_WIDTH].astype(BF16)
        wvt = wl[:, 2 * BRANCH_WIDTH:ATTN_COLS].T.astype(BF16)
        wg =wl[:, GATE_OFF:GATE_OFF + GATE_COLS].astype(BF16)
        wp = wl[:, POOL_OFF:POOL_OFF + BRANCH_WIDTH].astype(BF16)
        has_vmix = l > 0
        w_vm = w_vmix[l - 1] if has_vmix else jnp.zeros((D, VMIX_LORA), F32)
        mu_vm = vmix_mu[l - 1] if has_vmix else jnp.zeros((VMIX_LORA,), F32)
        wr = _rwkv_cols(wl[:, RWKV_OFF:], w_vm).astype(BF16)
        mu = _rwkv_cols(rwkv_mu[l][None, :], mu_vm[None, :])
        q, k, vt, gates, zp, zr, kmean = _inproj(xf, norm1_g[l][None, :], wa, wvt, wg, wp, wr)

        ids = _moba_select(q, kmean.reshape(B, nb, BRANCH_WIDTH), B, S)
        y_attn = _moba_attn(ids, q, k, vt, slope_cols, ccol, B, S)

        par = jnp.stack([w0[l], a0[l], k_k[l], k_a[l], r_k[l].reshape(-1),
                         v0[l - 1] if has_vmix else jnp.zeros((BRANCH_WIDTH,), F32),
                         jnp.zeros((BRANCH_WIDTH,), F32), jnp.zeros((BRANCH_WIDTH,), F32)])
        wvm_up = _pad_rows(w_vmix_up[l - 1], LANES) if has_vmix else jnp.zeros((LANES, BRANCH_WIDTH), F32)
        y_rwkv, v_cur = _rwkv(
            zr, v_first, mu, par, _pad_rows(w_decay_up[l], LANES), _pad_rows(w_iclr_up[l], LANES), w_gate_up[l],
            wvm_up, jnp.stack([lnx_g[l], lnx_b[l]]), has_vmix, B, S)
        if l == 0:
            v_first = v_cur

        y_pool = _pool(zp, pool_w[l].astype(BF16), pool_scale[l][None, :], B, S)

        rw = jnp.zeros((D, LANES), F32).at[:, :N_GROUPS].set(router_grp_w[l]).at[:, N_EXPERTS:2 * N_EXPERTS].set(router_exp_w[l])
        rw_hi = rw.astype(BF16)
        rw_lo = (rw - rw_hi.astype(F32)).astype(BF16)
        rb = jnp.zeros((1, LANES), F32).at[0, :N_GROUPS].set(router_grp_b[l]).at[0, N_EXPERTS:2 * N_EXPERTS].set(router_exp_b[l])
        x1, h2, eid, wts, cnt = _merge(xf, gates, y_attn, y_rwkv, y_pool, w_branch[l].astype(BF16),
                                       w_out[l].astype(BF16), norm2_g[l][None, :], jnp.stack([rw_hi, rw_lo]), rb)

        dest, pends = _moe_dest(eid, cnt)
        dest_flat = dest.reshape(-1)
        blk_row = jnp.arange(nblk, dtype=jnp.int32)[:, None] * MOE_BLOCK
        blk_e = jnp.minimum(jnp.sum((pends[0, :N_EXPERTS][None, :] <= blk_row).astype(jnp.int32), axis=1),
                            N_EXPERTS - 1)
        buf = _moe_scatter(dest_flat, h2, jnp.zeros((R, D), F32))
        yb = _moe_experts(blk_e, buf, exp_w1[l].astype(BF16), exp_w3[l].astype(BF16), exp_w2[l].astype(BF16))
        xf = _moe_combine(dest_flat, x1, wts, final_norm_g[None, :], yb, l == DEPTH - 1)
    return xf.reshape(B, S, D)
```

```python
import functools

import jax
import jax.numpy as jnp
from jax import lax
from jax.experimental import pallas as pl
from jax.experimental.pallas import tpu as pltpu

F32 = jnp.float32
BF16 = jnp.bfloat16
HIGHEST = lax.Precision.HIGHEST
LOG2E = 1.4426950408889634

D_MODEL = 1024
DEPTH = 2
BRANCH_WIDTH = 512
N_BRANCH = 3
ATTN_HEADS = 8
HEAD_DIM = 64
MOBA_BLOCK = 256
MOBA_TOPK = 3
DECAY_LORA = 64
ICLR_LORA = 64
GATE_LORA = 128
VMIX_LORA = 32
LNX_EPS = 64e-5
POOL_WINDOWS = (2, 4, 8, 16)
POOL_GROUP_DIM = 128
N_GROUPS = 4
EXPERTS_PER_GROUP = 8
N_EXPERTS = 32
EXPERT_TOPK = 2
EXPERT_HIDDEN = 512
RMS_EPS = 1e-6
ATTN_COLS = 3 * BRANCH_WIDTH
GATE_OFF = ATTN_COLS
GATE_COLS = N_BRANCH * D_MODEL
POOL_OFF = GATE_OFF + GATE_COLS
RWKV_OFF = POOL_OFF + BRANCH_WIDTH

LANES = 128
TM = 256
CHUNK = 64
N_CHUNK = TM // CHUNK
N_PAIR = ATTN_HEADS // 2
RW_COLS = 2048
POOL_HALO = 16
ATTN_GROUP = 4
MOE_BLOCK = 256
DMA_ISSUE_UNROLL = 8
V_ROWS = 80
VMEM_LIMIT = 56 * 1024 * 1024


def _cparams(sem):
    return pltpu.CompilerParams(dimension_semantics=sem, vmem_limit_bytes=VMEM_LIMIT)


def _const_spec(shape):
    nd = len(shape)
    return pl.BlockSpec(shape, lambda *a: (0,) * nd, pipeline_mode=pl.Buffered(1))


def _sigmoid(x):
    return 1.0 / (1.0 + jnp.exp(-x))


def _rms(x, g):
    return x * lax.rsqrt(jnp.mean(x * x, axis=-1, keepdims=True) + RMS_EPS) * g


def _split3(x):
    h1 = x.astype(BF16)
    r1 = x - h1.astype(F32)
    h2 = r1.astype(BF16)
    h3 = (r1 - h2.astype(F32)).astype(BF16)
    return h1, h2, h3


def _dot(a, b):
    return jnp.dot(a, b, preferred_element_type=F32)


def _dot_nt(a, b):
    return lax.dot_general(a, b, (((1,), (1,)), ((), ())), preferred_element_type=F32)


def _dot_tn(a, b):
    return lax.dot_general(a, b, (((0,), (0,)), ((), ())), preferred_element_type=F32)


def _inproj_body(x_ref, g_ref, wa_ref, wvt_ref, wg_ref, wp_ref, wr_ref,
                 q_ref, k_ref, vt_ref, gate_ref, zp_ref, zr_ref, km_ref):
    hb = _rms(x_ref[...], g_ref[...]).astype(BF16)
    za = _dot(hb, wa_ref[...])
    q_ref[...] = (za[:, :BRANCH_WIDTH] * (HEAD_DIM ** -0.5 * LOG2E)).astype(BF16)
    k = za[:, BRANCH_WIDTH:]
    k_ref[...] = k.astype(BF16)
    vt = _dot_nt(wvt_ref[...], hb)
    ones = jnp.ones((V_ROWS - HEAD_DIM, TM), BF16)
    for hd in range(ATTN_HEADS):
        vt_ref[0, hd // 2, hd % 2, 0:HEAD_DIM, :] = vt[hd * HEAD_DIM:(hd + 1) * HEAD_DIM].astype(BF16)
        vt_ref[0, hd // 2, hd % 2, HEAD_DIM:, :] = ones
    km_ref[0] = jnp.mean(k, axis=0, keepdims=True)
    gate_ref[...] = _sigmoid(_dot(hb, wg_ref[...]))
    zp_ref[...] = _dot(hb, wp_ref[...])
    zr_ref[...] = _dot(hb, wr_ref[...])


def _inproj(xf, g, wa, wvt, wg, wp, wr):
    T = xf.shape[0]
    nt = T // TM
    row = lambda w: pl.BlockSpec((TM, w), lambda i: (i, 0))
    return pl.pallas_call(
        _inproj_body,
        grid=(nt,),
        in_specs=[row(D_MODEL), _const_spec((1, D_MODEL)), _const_spec(wa.shape), _const_spec(wvt.shape),
                  _const_spec(wg.shape), _const_spec(wp.shape), _const_spec(wr.shape)],
        out_specs=[row(BRANCH_WIDTH), row(BRANCH_WIDTH),
                   pl.BlockSpec((1, N_PAIR, 2, V_ROWS, TM), lambda i: (i, 0, 0, 0, 0)),
                   row(GATE_COLS), row(BRANCH_WIDTH), row(RW_COLS),
                   pl.BlockSpec((1, 1, BRANCH_WIDTH), lambda i: (i, 0, 0))],
        out_shape=[jax.ShapeDtypeStruct((T, BRANCH_WIDTH), BF16), jax.ShapeDtypeStruct((T, BRANCH_WIDTH), BF16),
                   jax.ShapeDtypeStruct((nt, N_PAIR, 2, V_ROWS, TM), BF16),
                   jax.ShapeDtypeStruct((T, GATE_COLS), F32), jax.ShapeDtypeStruct((T, BRANCH_WIDTH), F32),
                   jax.ShapeDtypeStruct((T, RW_COLS), F32), jax.ShapeDtypeStruct((nt, 1, BRANCH_WIDTH), F32)],
        compiler_params=_cparams(("parallel",)),
        name="inproj",
    )(xf, g, wa, wvt, wg, wp, wr)


def _select_body(q_ref, km_ref, o_ref):
    i = pl.program_id(1)
    nb = km_ref.shape[1]
    lane = lax.broadcasted_iota(jnp.int32, (1, LANES), 1)
    blk = lax.broadcasted_iota(jnp.int32, (nb, TM), 0)
    for p in range(N_PAIR):
        q = q_ref[:, p * LANES:(p + 1) * LANES]
        km = km_ref[0, :, p * LANES:(p + 1) * LANES]
        rows = []
        for half in range(2):
            in_head = (lane < HEAD_DIM) if half == 0 else (lane >= HEAD_DIM)
            g = sum(_dot_nt(part, q) for part in _split3(jnp.where(in_head, km, 0.0)))
            g = jnp.where(blk < i, g, -jnp.inf)
            for _ in range(MOBA_TOPK):
                mx = jnp.max(g, axis=0, keepdims=True)
                idx = jnp.min(jnp.where(g == mx, blk, nb), axis=0, keepdims=True)
                rows.append(jnp.where(mx > -jnp.inf, idx, -1))
                g = jnp.where(blk == idx, -jnp.inf, g)
        rows += [jnp.full((1, TM), -1, jnp.int32)] * 2
        o_ref[p] = jnp.concatenate(rows, axis=0)


def _moba_select(q, kmean, B, S):
    ns = S // TM
    nb = kmean.shape[1]
    return pl.pallas_call(
        _select_body,
        grid=(B, ns),
        in_specs=[pl.BlockSpec((TM, BRANCH_WIDTH), lambda b, i: (b * ns + i, 0)),
                  pl.BlockSpec((1, nb, BRANCH_WIDTH), lambda b, i: (b, 0, 0))],
        out_specs=pl.BlockSpec((N_PAIR, 8, TM), lambda b, i: (b, 0, i)),
        out_shape=jax.ShapeDtypeStruct((B * N_PAIR, 8, S), jnp.int32),
        compiler_params=_cparams(("parallel", "parallel")),
        name="moba_select",
    )(q, kmean)


def _attn_body(ids_ref, q_ref, k_ref, vt_ref, slope_ref, ccol_ref, o_ref, qa_ref, m_ref, acc_ref, sa_ref, sb_ref,
               mxa_ref, mxb_ref, pa_ref, pb_ref):
    i = pl.program_id(2)
    lane = lax.broadcasted_iota(jnp.int32, (1, LANES), 1)
    first = lane < HEAD_DIM
    q = q_ref[...]
    zero = jnp.zeros_like(q)
    qa_ref[0, :, 0:LANES] = jnp.where(first, q, zero)
    qa_ref[1, :, 0:LANES] = jnp.where(first, zero, q)
    qa_ref[:, :, LANES:] = slope_ref[0]
    ids = ids_ref[0]

    def scores(j0, nblk):
        kj = k_ref[pl.ds(pl.multiple_of(j0 * TM, TM), nblk * TM), :]
        kaug = jnp.concatenate([kj, ccol_ref[0:nblk * TM, :]], axis=1)
        return [_dot_nt(kaug, qa_ref[h]) for h in range(2)]

    def pv(h, p, j0, nblk):
        vaug = jnp.concatenate([vt_ref[j0 + u, 0, h] for u in range(nblk)], axis=1)
        return _dot(vaug, p)

    kidx = lax.broadcasted_iota(jnp.int32, (TM, TM), 0)
    qidx = lax.broadcasted_iota(jnp.int32, (TM, TM), 1)
    def own_block():
        for h, s in enumerate(scores(i, 1)):
            s = jnp.where(kidx <= qidx, s, -jnp.inf)
            m0 = jnp.max(s, axis=0, keepdims=True)
            m_ref[h] = m0
            acc_ref[h] = pv(h, jnp.exp2(s - m0).astype(BF16), i, 1)

    def score_block(g, u, buf, mx):
        kj = k_ref[pl.ds(pl.multiple_of((g * ATTN_GROUP + u) * TM, TM), TM), :]
        kaug = jnp.concatenate([kj, ccol_ref[u * TM:(u + 1) * TM, :]], axis=1)
        for h in range(2):
            s = _dot_nt(kaug, qa_ref[h])
            buf[h, u * TM:(u + 1) * TM, :] = s
            parts = [s[r:r + 8] for r in range(0, 64, 8)]
            for r in range(64, TM, 8):
                parts[(r // 8) % 8] = jnp.maximum(parts[(r // 8) % 8], s[r:r + 8])
            while len(parts) > 1:
                parts = [jnp.maximum(parts[n], parts[n + 1]) for n in range(0, len(parts), 2)]
            mx[h, u] = parts[0]

    def stage(g, cur, nxt, g_next):
        cur_s, cur_mx, cur_p = cur
        j0 = g * ATTN_GROUP
        dist = ((j0 - i) * TM).astype(F32)
        stats = []
        for h in range(2):
            slope = slope_ref[0, h, 0:1, 0:1].astype(F32) + slope_ref[0, h, 0:1, 1:2].astype(F32)
            c0 = slope * dist
            r0 = h * MOBA_TOPK
            sels, best = [], None
            for u in range(ATTN_GROUP):
                j = j0 + u
                sel = (ids[r0:r0 + 1] == j) | (ids[r0 + 1:r0 + 2] == j) | (ids[r0 + 2:r0 + 3] == j)
                cand = jnp.where(sel, cur_mx[h, u], -jnp.inf)
                best = cand if best is None else jnp.maximum(best, cand)
                sels.append(sel)
            m_old = m_ref[h]
            m_new = jnp.maximum(m_old, jnp.max(best, axis=0, keepdims=True) + c0)
            m_ref[h] = m_new
            stats.append((jnp.exp2(m_old - m_new), [jnp.where(sel, m_new - c0, jnp.inf) for sel in sels]))
        for u in range(ATTN_GROUP):
            if nxt is not None:
                score_block(g_next, u, nxt[0], nxt[1])
            for h in range(2):
                cur_p[h, u * TM:(u + 1) * TM, :] = jnp.exp2(cur_s[h, u * TM:(u + 1) * TM, :] - stats[h][1][u]).astype(BF16)
        for h in range(2):
            acc_ref[h] = acc_ref[h] * stats[h][0] + pv(h, cur_p[h], j0, ATTN_GROUP)

    n_group = (i + ATTN_GROUP - 1) // ATTN_GROUP
    buf_a, buf_b = (sa_ref, mxa_ref, pa_ref), (sb_ref, mxb_ref, pb_ref)

    for u in range(ATTN_GROUP):
        score_block(0, u, sa_ref, mxa_ref)
    own_block()

    def pair_body(t, carry):
        stage(2 * t, buf_a, buf_b, 2 * t + 1)
        stage(2 * t + 1, buf_b, buf_a, jnp.minimum(2 * t + 2, n_group - 1))
        return carry

    lax.fori_loop(0, n_group // 2, pair_body, 0)

    @pl.when(n_group % 2 == 1)
    def _():
        stage(n_group - 1, buf_a, None, None)

    acc_a, acc_b = acc_ref[0], acc_ref[1]
    out_t = jnp.concatenate([acc[:HEAD_DIM] / acc[HEAD_DIM:HEAD_DIM + 1] for acc in (acc_a, acc_b)], axis=0)
    o_ref[...] = out_t.T.astype(BF16)


def _moba_attn(ids, q, k, vt, slope_cols, ccol, B, S):
    ns = S // TM
    return pl.pallas_call(
        _attn_body,
        grid=(B, N_PAIR, ns),
        in_specs=[pl.BlockSpec((1, 8, TM), lambda b, p, i: (b * N_PAIR + p, 0, i)),
                  pl.BlockSpec((TM, LANES), lambda b, p, i: (b * ns + i, p)),
                  pl.BlockSpec((S, LANES), lambda b, p, i: (b, p)),
                  pl.BlockSpec((ns, 1, 2, V_ROWS, TM), lambda b, p, i: (b, p, 0, 0, 0)),
                  pl.BlockSpec((1, 2, TM, LANES), lambda b, p, i: (p, 0, 0, 0)),
                  pl.BlockSpec((ATTN_GROUP * TM, LANES), lambda b, p, i: (0, 0))],
        out_specs=pl.BlockSpec((TM, LANES), lambda b, p, i: (b * ns + i, p)),
        out_shape=jax.ShapeDtypeStruct((B * S, BRANCH_WIDTH), BF16),
        scratch_shapes=[pltpu.VMEM((2, TM, 2 * LANES), BF16), pltpu.VMEM((2, 1, TM), F32),
                        pltpu.VMEM((2, V_ROWS, TM), F32),
                        pltpu.VMEM((2, ATTN_GROUP * TM, TM), F32), pltpu.VMEM((2, ATTN_GROUP * TM, TM), F32),
                        pltpu.VMEM((2, ATTN_GROUP, 8, TM), F32), pltpu.VMEM((2, ATTN_GROUP, 8, TM), F32),
                        pltpu.VMEM((2, ATTN_GROUP * TM, TM), BF16), pltpu.VMEM((2, ATTN_GROUP * TM, TM), BF16)],
        compiler_params=_cparams(("parallel", "parallel", "arbitrary")),
        name="moba_attn",
    )(ids, q, k, vt, slope_cols, ccol)


def _pool_body(z_ref, halo_ref, w_ref, sc_ref, o_ref, ext_ref):
    i = pl.program_id(1)
    ext_ref[0:POOL_HALO, :] = jnp.where(i == 0, 0.0, halo_ref[...])
    ext_ref[POOL_HALO:, :] = z_ref[...]
    t1 = (i * TM + 1 + lax.broadcasted_iota(jnp.int32, (TM, 1), 0)).astype(F32)
    for gi, win in enumerate(POOL_WINDOWS):
        cols = slice(gi * POOL_GROUP_DIM, (gi + 1) * POOL_GROUP_DIM)
        acc = ext_ref[POOL_HALO:, cols]
        for d in range(1, win):
            acc = acc + ext_ref[POOL_HALO - d:POOL_HALO - d + TM, cols]
        p = acc / jnp.minimum(t1, float(win)) - ext_ref[POOL_HALO:, cols]
        y = _dot(p.astype(BF16), w_ref[gi])
        o_ref[:, cols] = (y * sc_ref[:, cols]).astype(BF16)


def _pool(zp, w, sc, B, S):
    ns = S // TM
    hb = TM // POOL_HALO
    return pl.pallas_call(
        _pool_body,
        grid=(B, ns),
        in_specs=[pl.BlockSpec((TM, BRANCH_WIDTH), lambda b, i: (b * ns + i, 0)),
                  pl.BlockSpec((POOL_HALO, BRANCH_WIDTH), lambda b, i: (jnp.maximum((b * ns + i) * hb - 1, 0), 0)),
                  _const_spec(w.shape), _const_spec(sc.shape)],
        out_specs=pl.BlockSpec((TM, BRANCH_WIDTH), lambda b, i: (b * ns + i, 0)),
        out_shape=jax.ShapeDtypeStruct((B * S, BRANCH_WIDTH), BF16),
        scratch_shapes=[pltpu.VMEM((POOL_HALO + TM, BRANCH_WIDTH), F32)],
        compiler_params=_cparams(("parallel", "parallel")),
        name="pool",
    )(zp, zp, w, sc)


def _seg_sum(x, seg):
    hi = x.astype(BF16)
    lo = (x - hi.astype(F32)).astype(BF16)
    outs = []
    for p in range(N_PAIR):
        cols = slice(p * LANES, (p + 1) * LANES)
        outs.append(_dot(hi[:, cols], seg) + _dot(lo[:, cols], seg))
    return jnp.concatenate(outs, axis=1)


def _rwkv_body(has_vmix, z_ref, halo_ref, vf_ref, mu_ref, par_ref, wd_ref, wa_ref, wg_ref, wvm_ref, ln_ref,
               o_ref, v_ref, s_ref):
    i = pl.program_id(1)

    @pl.when(i == 0)
    def _():
        s_ref[...] = jnp.zeros_like(s_ref)

    W = BRANCH_WIDTH
    z = z_ref[...]
    prev0 = jnp.where(i == 0, 0.0, halo_ref[7:8, :])
    row = lax.broadcasted_iota(jnp.int32, (TM, 1), 0)
    prev = jnp.where(row == 0, prev0, pltpu.roll(z, 1, axis=0))
    rw = z + (prev - z) * mu_ref[...]
    r, k, v = rw[:, 0:W], rw[:, W:2 * W], rw[:, 2 * W:3 * W]
    w_lo = rw[:, 3 * W:3 * W + LANES]
    a_lo = rw[:, 3 * W + LANES:3 * W + 2 * LANES]
    g_lo = rw[:, 3 * W + 2 * LANES:3 * W + 3 * LANES]
    vm_lo = rw[:, 3 * W + 3 * LANES:3 * W + 4 * LANES]
    w0, a0, k_k, k_a, r_k, v0 = (par_ref[n:n + 1, :] for n in range(6))
    hdot = functools.partial(jnp.dot, precision=HIGHEST, preferred_element_type=F32)

    u = w0 + hdot(jnp.tanh(w_lo), wd_ref[...])
    nu = -u
    softplus = jnp.maximum(nu, 0.0) + jnp.log(1.0 + jnp.exp(-jnp.abs(nu)))
    logw = -jnp.exp(-softplus - 0.5)
    if has_vmix:
        v = v + (vf_ref[...] - v) * _sigmoid(v0 + hdot(vm_lo, wvm_ref[...]))
    v_ref[...] = v
    a = _sigmoid(a0 + hdot(a_lo, wa_ref[...]))
    gate = hdot(_sigmoid(g_lo), wg_ref[...])

    lane = lax.broadcasted_iota(jnp.int32, (1, LANES), 1)
    first = lane < HEAD_DIM
    seg = ((lax.broadcasted_iota(jnp.int32, (LANES, LANES), 0) < HEAD_DIM)
           == (lax.broadcasted_iota(jnp.int32, (LANES, LANES), 1) < HEAD_DIM)).astype(BF16)
    kk = k * k_k
    kk = kk / jnp.maximum(jnp.sqrt(_seg_sum(kk * kk, seg)), 1e-12)
    k2 = k * (1.0 + (a - 1.0) * k_a)
    bonus = _seg_sum(r * k2 * r_k, seg) * v

    ri = lax.broadcasted_iota(jnp.int32, (TM, TM), 0)
    ci = lax.broadcasted_iota(jnp.int32, (TM, TM), 1)
    same = (ri // CHUNK) == (ci // CHUNK)
    incl = same & (ri >= ci)
    strict = same & (ri > ci)
    lmat = incl.astype(BF16)
    h1, h2, h3 = _split3(logw)
    cw = _dot(lmat, h1) + _dot(lmat, h2) + _dot(lmat, h3)
    e_neg = jnp.exp(-cw)
    at = (-kk) * jnp.exp(cw - logw)
    bt = (kk * a) * e_neg
    kt = k2 * e_neg
    rt = r * jnp.exp(cw)
    eye_pk = (lax.broadcasted_iota(jnp.int32, (CHUNK, TM), 0)
              == lax.broadcasted_iota(jnp.int32, (CHUNK, TM), 1) % CHUNK).astype(F32)
    bd = ((lax.broadcasted_iota(jnp.int32, (LANES, LANES), 0) < HEAD_DIM)
          == (lax.broadcasted_iota(jnp.int32, (LANES, LANES), 1) < HEAD_DIM))
    eye_l = (lax.broadcasted_iota(jnp.int32, (LANES, LANES), 0)
             == lax.broadcasted_iota(jnp.int32, (LANES, LANES), 1)).astype(F32)

    def block_diag(packed):
        return jnp.where(same, jnp.concatenate([packed] * N_CHUNK, axis=0), jnp.zeros((), packed.dtype))

    heads = []
    for hd in range(ATTN_HEADS):
        cols = slice((hd // 2) * LANES, (hd // 2 + 1) * LANES)
        mh = first if hd % 2 == 0 else jnp.logical_not(first)
        am = jnp.where(mh, at[:, cols], 0.0)
        rm = jnp.where(mh, rt[:, cols], 0.0)
        vm = jnp.where(mh, v[:, cols], 0.0).astype(BF16)
        bk = jnp.concatenate([bt[:, cols], kt[:, cols]], axis=0).astype(BF16)
        big = _dot_nt(jnp.concatenate([am, rm], axis=0).astype(BF16), bk)
        n = jnp.where(strict, big[:TM, :TM], 0.0)
        n_pk = n[0:CHUNK]
        for c in range(1, N_CHUNK):
            n_pk = n_pk + n[c * CHUNK:(c + 1) * CHUNK]
        heads.append(dict(
            am=am, rm=rm, vm=vm, t_pk=eye_pk + n_pk, p_pk=n_pk, p_bd=n.astype(BF16),
            aak=jnp.where(strict, big[:TM, TM:], 0.0).astype(BF16),
            mrb=jnp.where(incl, big[TM:, :TM], 0.0).astype(BF16),
            mrk=jnp.where(incl, big[TM:, TM:], 0.0).astype(BF16)))
    for _ in range(CHUNK.bit_length() - 2):
        for hs in heads:
            hs["p_pk"] = _dot(hs["p_pk"].astype(BF16), hs["p_bd"])
        for hs in heads:
            hs["p_bd"] = block_diag(hs["p_pk"].astype(BF16))
            hs["t_pk"] = hs["t_pk"] + _dot(hs["t_pk"].astype(BF16), hs["p_bd"])
    for hs in heads:
        hs["av"] = _dot(hs["aak"], hs["vm"])
    for hs in heads:
        x = _dot(block_diag(hs["t_pk"].astype(BF16)),
                 jnp.concatenate([hs["am"], hs["av"]], axis=1).astype(BF16))
        hs["ph"], hs["qh"] = x[:, :LANES], x[:, LANES:]
    for hs in heads:
        hs["rp"] = hs["rm"] + _dot(hs["mrb"], hs["ph"].astype(BF16))
        hs["y0"] = _dot(hs["mrb"], hs["qh"].astype(BF16)) + _dot(hs["mrk"], hs["vm"])

    pairs = []
    for p in range(N_PAIR):
        cols = slice(p * LANES, (p + 1) * LANES)
        ha, hb = heads[2 * p], heads[2 * p + 1]
        bp, kp, vp = bt[:, cols], kt[:, cols], v[:, cols]
        p_pair = ha["ph"] + hb["ph"]
        q_pair = ha["qh"] + hb["qh"]
        gmats, hmats = [], []
        for c in range(N_CHUNK):
            rows = slice(c * CHUNK, (c + 1) * CHUNK)
            wc = jnp.exp(cw[(c + 1) * CHUNK - 1:(c + 1) * CHUNK, cols])
            bc = bp[rows].astype(BF16)
            gp = _dot_tn(p_pair[rows].astype(BF16), bc)
            hp = _dot_tn(jnp.concatenate([q_pair[rows], vp[rows]], axis=0).astype(BF16),
                         jnp.concatenate([bp[rows], kp[rows]], axis=0).astype(BF16))
            gmats.append(((eye_l + jnp.where(bd, gp, 0.0)) * wc).astype(BF16))
            hmats.append(jnp.where(bd, hp, 0.0) * wc)
        pairs.append(dict(rp=(ha["rp"] + hb["rp"]).astype(BF16), y0=ha["y0"] + hb["y0"], g=gmats, h=hmats))

    states = [s_ref[p] for p in range(N_PAIR)]
    yc = [[] for _ in range(N_PAIR)]
    for c in range(N_CHUNK):
        rows = slice(c * CHUNK, (c + 1) * CHUNK)
        for p, pr in enumerate(pairs):
            s16 = states[p].astype(BF16)
            yc[p].append(_dot_nt(pr["rp"][rows], s16) + pr["y0"][rows])
            states[p] = _dot(s16, pr["g"][c]) + pr["h"][c]
    for p in range(N_PAIR):
        s_ref[p] = states[p]
    y = jnp.concatenate([jnp.concatenate(yc[p], axis=0) for p in range(N_PAIR)], axis=1)
    mean = _seg_sum(y, seg) * (1.0 / HEAD_DIM)
    d = y - mean
    var = _seg_sum(d * d, seg) * (1.0 / HEAD_DIM)
    yn = d * lax.rsqrt(var + LNX_EPS) * ln_ref[0:1, :] + ln_ref[1:2, :]
    o_ref[...] = ((yn + bonus) * gate).astype(BF16)


def _rwkv(zr, vfirst, mu, par, wd, wa, wg, wvm, ln, has_vmix, B, S):
    ns = S // TM
    T = B * S
    tile = lambda w: pl.BlockSpec((TM, w), lambda b, i: (b * ns + i, 0))
    return pl.pallas_call(
        functools.partial(_rwkv_body, has_vmix),
        grid=(B, ns),
        in_specs=[tile(RW_COLS),
                  pl.BlockSpec((8, RW_COLS), lambda b, i: (jnp.maximum((b * ns + i) * (TM // 8) - 1, 0), 0)),
                  tile(BRANCH_WIDTH), _const_spec(mu.shape), _const_spec(par.shape), _const_spec(wd.shape),
                  _const_spec(wa.shape), _const_spec(wg.shape), _const_spec(wvm.shape), _const_spec(ln.shape)],
        out_specs=[tile(BRANCH_WIDTH), tile(BRANCH_WIDTH)],
        out_shape=[jax.ShapeDtypeStruct((T, BRANCH_WIDTH), BF16), jax.ShapeDtypeStruct((T, BRANCH_WIDTH), F32)],
        scratch_shapes=[pltpu.VMEM((N_PAIR, LANES, LANES), F32)],
        compiler_params=_cparams(("parallel", "arbitrary")),
        name="rwkv",
    )(zr, zr, vfirst, mu, par, wd, wa, wg, wvm, ln)


def _merge_body(x_ref, gate_ref, ya_ref, yr_ref, yp_ref, wb_ref, wo_ref, g2_ref, rw_ref, rb_ref,
                x1_ref, h2_ref, eid_ref, wt_ref, cnt_ref):
    merged = jnp.zeros((TM, D_MODEL), F32)
    for bi, y_ref in enumerate((ya_ref, yr_ref, yp_ref)):
        merged = merged + gate_ref[:, bi * D_MODEL:(bi + 1) * D_MODEL] * _dot(y_ref[...], wb_ref[bi])
    x1 = x_ref[...] + _dot(merged.astype(BF16), wo_ref[...])
    x1_ref[...] = x1
    h2 = _rms(x1, g2_ref[...])
    h2_ref[...] = h2
    hi = h2.astype(BF16)
    lo = (h2 - hi.astype(F32)).astype(BF16)
    logits = _dot(hi, rw_ref[0]) + _dot(lo, rw_ref[0]) + _dot(hi, rw_ref[1]) + rb_ref[...]
    lane = lax.broadcasted_iota(jnp.int32, (TM, LANES), 1)
    ninf = -jnp.inf
    glog = jnp.where(lane < N_GROUPS, logits, ninf)
    gmax = jnp.max(glog, axis=-1, keepdims=True)
    grp = jnp.min(jnp.where(glog == gmax, lane, LANES), axis=-1, keepdims=True)
    gprob = 1.0 / jnp.sum(jnp.exp(glog - gmax), axis=-1, keepdims=True)
    e_lane = lane - N_EXPERTS
    in_grp = (e_lane >= grp * EXPERTS_PER_GROUP) & (e_lane < (grp + 1) * EXPERTS_PER_GROUP)
    elog = jnp.where(in_grp, logits, ninf)
    v1 = jnp.max(elog, axis=-1, keepdims=True)
    i1 = jnp.min(jnp.where(elog == v1, e_lane, LANES), axis=-1, keepdims=True)
    elog = jnp.where(e_lane == i1, ninf, elog)
    v2 = jnp.max(elog, axis=-1, keepdims=True)
    i2 = jnp.min(jnp.where(elog == v2, e_lane, LANES), axis=-1, keepdims=True)
    e2 = jnp.exp(v2 - v1)
    den = 1.0 + e2
    eid_ref[...] = jnp.concatenate([i1, i2], axis=1)
    wt_ref[...] = jnp.concatenate([gprob / den, gprob * e2 / den], axis=1)
    hit = ((lane == i1) | (lane == i2)).astype(F32)
    cnt_ref[0] = jnp.sum(hit, axis=0, keepdims=True).astype(jnp.int32)


def _merge(xf, gates, ya, yr, yp, wb, wo, g2, rw, rb):
    T = xf.shape[0]
    nt = T // TM
    row = lambda w: pl.BlockSpec((TM, w), lambda i: (i, 0))
    return pl.pallas_call(
        _merge_body,
        grid=(nt,),
        in_specs=[row(D_MODEL), row(GATE_COLS), row(BRANCH_WIDTH), row(BRANCH_WIDTH), row(BRANCH_WIDTH),
                  _const_spec(wb.shape), _const_spec(wo.shape), _const_spec(g2.shape), _const_spec(rw.shape),
                  _const_spec(rb.shape)],
        out_specs=[row(D_MODEL), row(D_MODEL), row(EXPERT_TOPK), row(EXPERT_TOPK),
                   pl.BlockSpec((1, 1, LANES), lambda i: (i, 0, 0))],
        out_shape=[jax.ShapeDtypeStruct((T, D_MODEL), F32), jax.ShapeDtypeStruct((T, D_MODEL), F32),
                   jax.ShapeDtypeStruct((T, EXPERT_TOPK), jnp.int32), jax.ShapeDtypeStruct((T, EXPERT_TOPK), F32),
                   jax.ShapeDtypeStruct((nt, 1, LANES), jnp.int32)],
        compiler_params=_cparams(("parallel",)),
        name="merge",
    )(xf, gates, ya, yr, yp, wb, wo, g2, rw, rb)


def _lane_excl_cumsum(x):
    lane = lax.broadcasted_iota(jnp.int32, x.shape, 1)
    inc = x
    sh = 1
    while sh < LANES:
        inc = inc + jnp.where(lane >= sh, pltpu.roll(inc, sh, axis=1), 0)
        sh *= 2
    return inc - x


def _dest_body(eid_ref, cnt_ref, dest_ref, pend_ref, base_ref):
    i = pl.program_id(0)

    @pl.when(i == 0)
    def _():
        total = jnp.sum(cnt_ref[...], axis=0)
        shift = MOE_BLOCK.bit_length() - 1
        padded = jnp.left_shift(jnp.right_shift(total + (MOE_BLOCK - 1), shift), shift)
        pstart = _lane_excl_cumsum(padded)
        base_ref[...] = pstart
        pend_ref[...] = pstart + padded

    lane = lax.broadcasted_iota(jnp.int32, (TM, LANES), 1)
    e0, e1 = eid_ref[:, 0:1], eid_ref[:, 1:2]
    oh0, oh1 = lane == e0, lane == e1
    both = (oh0 | oh1).astype(BF16)
    lower = (lax.broadcasted_iota(jnp.int32, (TM, TM), 0)
             > lax.broadcasted_iota(jnp.int32, (TM, TM), 1)).astype(BF16)
    pos = _dot(lower, both) + base_ref[...].astype(F32)
    d0 = jnp.sum(jnp.where(oh0, pos, 0.0), axis=-1, keepdims=True)
    d1 = jnp.sum(jnp.where(oh1, pos, 0.0), axis=-1, keepdims=True)
    dest_ref[...] = jnp.concatenate([d0, d1], axis=1).astype(jnp.int32)
    base_ref[...] = base_ref[...] + jnp.sum(both.astype(F32), axis=0, keepdims=True).astype(jnp.int32)


def _moe_dest(eid, cnt):
    T = eid.shape[0]
    nt = T // TM
    return pl.pallas_call(
        _dest_body,
        grid=(nt,),
        in_specs=[pl.BlockSpec((TM, EXPERT_TOPK), lambda i: (i, 0)),
                  pl.BlockSpec((nt, 1, LANES), lambda i: (0, 0, 0))],
        out_specs=[pl.BlockSpec((TM, EXPERT_TOPK), lambda i: (i, 0)),
                   pl.BlockSpec((1, LANES), lambda i: (0, 0))],
        out_shape=[jax.ShapeDtypeStruct((T, EXPERT_TOPK), jnp.int32), jax.ShapeDtypeStruct((1, LANES), jnp.int32)],
        scratch_shapes=[pltpu.VMEM((1, LANES), jnp.int32)],
        compiler_params=_cparams(("arbitrary",)),
        name="moe_dest",
    )(eid, cnt)


def _scatter_body(dest_ref, h_ref, buf_in_ref, buf_ref, sem):
    del buf_in_ref
    i = pl.program_id(0)

    def issue(t, carry):
        for s in range(EXPERT_TOPK):
            d = dest_ref[(i * TM + t) * EXPERT_TOPK + s]
            pltpu.make_async_copy(h_ref.at[pl.ds(t, 1), :], buf_ref.at[pl.ds(d, 1), :], sem).start(priority=s)
        return carry

    lax.fori_loop(0, TM, issue, 0, unroll=DMA_ISSUE_UNROLL)
    for _ in range(EXPERT_TOPK):
        pltpu.make_async_copy(h_ref, buf_ref.at[pl.ds(0, TM), :], sem).wait()


def _moe_scatter(dest_flat, h2, buf0):
    T = h2.shape[0]
    nt = T // TM
    return pl.pallas_call(
        _scatter_body,
        grid_spec=pltpu.PrefetchScalarGridSpec(
            num_scalar_prefetch=1, grid=(nt,),
            in_specs=[pl.BlockSpec((TM, D_MODEL), lambda i, d: (i, 0)), pl.BlockSpec(memory_space=pl.ANY)],
            out_specs=pl.BlockSpec(memory_space=pl.ANY),
            scratch_shapes=[pltpu.SemaphoreType.DMA(())]),
        out_shape=jax.ShapeDtypeStruct(buf0.shape, F32),
        input_output_aliases={2: 0},
        compiler_params=_cparams(("arbitrary",)),
        name="moe_scatter",
    )(dest_flat, h2, buf0)


def _expert_body(be_ref, x_ref, w1_ref, w3_ref, w2_ref, o_ref):
    del be_ref
    xb = x_ref[...].astype(BF16)
    h1 = _dot(xb, w1_ref[0])
    h3 = _dot(xb, w3_ref[0])
    hid = (h1 * _sigmoid(h1)) * h3
    o_ref[...] = _dot(hid.astype(BF16), w2_ref[0])


def _moe_experts(blk_e, buf, w1, w3, w2):
    R = buf.shape[0]
    nblk = R // MOE_BLOCK
    return pl.pallas_call(
        _expert_body,
        grid_spec=pltpu.PrefetchScalarGridSpec(
            num_scalar_prefetch=1, grid=(nblk,),
            in_specs=[pl.BlockSpec((MOE_BLOCK, D_MODEL), lambda b, e: (b, 0)),
                      pl.BlockSpec((1, D_MODEL, EXPERT_HIDDEN), lambda b, e: (e[b], 0, 0)),
                      pl.BlockSpec((1, D_MODEL, EXPERT_HIDDEN), lambda b, e: (e[b], 0, 0)),
                      pl.BlockSpec((1, EXPERT_HIDDEN, D_MODEL), lambda b, e: (e[b], 0, 0))],
            out_specs=pl.BlockSpec((MOE_BLOCK, D_MODEL), lambda b, e: (b, 0))),
        out_shape=jax.ShapeDtypeStruct((R, D_MODEL), F32),
        compiler_params=_cparams(("arbitrary",)),
        name="moe_experts",
    )(blk_e, buf, w1, w3, w2)


def _combine_body(final, dest_ref, x_ref, wt_ref, gf_ref, yb_ref, o_ref, g_ref, sem):
    i = pl.program_id(0)

    def issue_tile(tile, slot):
        def issue(t, carry):
            for s in range(EXPERT_TOPK):
                d = dest_ref[(tile * TM + t) * EXPERT_TOPK + s]
                pltpu.make_async_copy(yb_ref.at[pl.ds(d, 1), :], g_ref.at[slot, s, pl.ds(t, 1), :],
                                      sem.at[slot]).start(priority=s)
            return carry

        lax.fori_loop(0, TM, issue, 0, unroll=DMA_ISSUE_UNROLL)

    @pl.when(i == 0)
    def _():
        issue_tile(0, 0)

    @pl.when(i + 1 < pl.num_programs(0))
    def _():
        issue_tile(i + 1, (i + 1) % 2)

    slot = i % 2
    for s in range(EXPERT_TOPK):
        pltpu.make_async_copy(yb_ref.at[pl.ds(0, TM), :], g_ref.at[slot, s], sem.at[slot]).wait()
    wt = wt_ref[...]
    x = x_ref[...] + (g_ref[slot, 0] * wt[:, 0:1] + g_ref[slot, 1] * wt[:, 1:2])
    if final:
        x = _rms(x, gf_ref[...])
    o_ref[...] = x


def _moe_combine(dest_flat, x1, wts, gf, yb, final):
    T = x1.shape[0]
    nt = T // TM
    return pl.pallas_call(
        functools.partial(_combine_body, final),
        grid_spec=pltpu.PrefetchScalarGridSpec(
            num_scalar_prefetch=1, grid=(nt,),
            in_specs=[pl.BlockSpec((TM, D_MODEL), lambda i, d: (i, 0)),
                      pl.BlockSpec((TM, EXPERT_TOPK), lambda i, d: (i, 0)),
                      pl.BlockSpec((1, D_MODEL), lambda i, d: (0, 0)),
                      pl.BlockSpec(memory_space=pl.ANY)],
            out_specs=pl.BlockSpec((TM, D_MODEL), lambda i, d: (i, 0)),
            scratch_shapes=[pltpu.VMEM((2, EXPERT_TOPK, TM, D_MODEL), F32), pltpu.SemaphoreType.DMA((2,))]),
        out_shape=jax.ShapeDtypeStruct((T, D_MODEL), F32),
        compiler_params=_cparams(("arbitrary",)),
        name="moe_combine",
    )(dest_flat, x1, wts, gf, yb)


def _pad_cols(w, n):
    return jnp.pad(w, ((0, 0), (0, n - w.shape[1])))


def _rwkv_cols(w_rw, w_vm):
    W = BRANCH_WIDTH
    o1, o2, o3, o4 = 3 * W, 3 * W + DECAY_LORA, 3 * W + DECAY_LORA + ICLR_LORA, 3 * W + DECAY_LORA + ICLR_LORA + GATE_LORA
    return jnp.concatenate([w_rw[:, :o1], _pad_cols(w_rw[:, o1:o2], LANES), _pad_cols(w_rw[:, o2:o3], LANES),
                            w_rw[:, o3:o4], _pad_cols(w_vm, LANES)], axis=1)


def _pad_rows(w, n):
    return jnp.pad(w, ((0, n - w.shape[0]), (0, 0)))


def kernel(x, norm1_g, w_in, w_vmix, rwkv_mu, vmix_mu, w0, w_decay_up, a0, w_iclr_up, w_gate_up, k_k, k_a, r_k,
           lnx_g, lnx_b, v0, w_vmix_up, pool_w, pool_scale, w_branch, w_out, norm2_g, router_grp_w, router_grp_b,
           router_exp_w, router_exp_b, exp_w1, exp_w3, exp_w2, final_norm_g):
    B, S, D = x.shape
    assert D == D_MODEL and S % (TM * ATTN_GROUP) == 0
    T = B * S
    ns = S // TM
    nb = ns
    xf = x.reshape(T, D)

    slopes = jnp.asarray([2.0 ** (-8.0 * (h + 1) / ATTN_HEADS) * LOG2E for h in range(ATTN_HEADS)], F32)
    slopes = slopes.reshape(N_PAIR, 2, 1)
    s_hi = slopes.astype(BF16).astype(F32)
    s_lo = slopes - s_hi
    slope_cols = jnp.zeros((N_PAIR, 2, TM, LANES), F32)
    for col, val in enumerate((s_hi, s_lo, s_hi * TM, s_lo * TM)):
        slope_cols = slope_cols.at[:, :, :, col].set(val)
    slope_cols = slope_cols.astype(BF16)
    key_row = jnp.arange(ATTN_GROUP * TM, dtype=jnp.int32)
    in_blk, blk_no = (key_row % TM).astype(F32), (key_row // TM).astype(F32)
    ccol = jnp.zeros((ATTN_GROUP * TM, LANES), F32)
    for col, val in enumerate((in_blk, in_blk, blk_no, blk_no)):
        ccol = ccol.at[:, col].set(val)
    ccol = ccol.astype(BF16)

    n_assign = T * EXPERT_TOPK
    R = -(-(n_assign + N_EXPERTS * (MOE_BLOCK - 1)) // MOE_BLOCK) * MOE_BLOCK
    nblk = R // MOE_BLOCK

    v_first = jnp.zeros((T, BRANCH_WIDTH), F32)
    for l in range(DEPTH):
        wl = w_in[l]
        wa = wl[:, :2 * BRANCH_WIDTH].astype(BF16)
        wvt = wl[:, 2 * BRANCH_WIDTH:ATTN_COLS].T.astype(BF16)
        wg =wl[:, GATE_OFF:GATE_OFF + GATE_COLS].astype(BF16)
        wp = wl[:, POOL_OFF:POOL_OFF + BRANCH_WIDTH].astype(BF16)
        has_vmix = l > 0
        w_vm = w_vmix[l - 1] if has_vmix else jnp.zeros((D, VMIX_LORA), F32)
        mu_vm = vmix_mu[l - 1] if has_vmix else jnp.zeros((VMIX_LORA,), F32)
        wr = _rwkv_cols(wl[:, RWKV_OFF:], w_vm).astype(BF16)
        mu = _rwkv_cols(rwkv_mu[l][None, :], mu_vm[None, :])
        q, k, vt, gates, zp, zr, kmean = _inproj(xf, norm1_g[l][None, :], wa, wvt, wg, wp, wr)

        ids = _moba_select(q, kmean.reshape(B, nb, BRANCH_WIDTH), B, S)
        y_attn = _moba_attn(ids, q, k, vt, slope_cols, ccol, B, S)

        par = jnp.stack([w0[l], a0[l], k_k[l], k_a[l], r_k[l].reshape(-1),
                         v0[l - 1] if has_vmix else jnp.zeros((BRANCH_WIDTH,), F32),
                         jnp.zeros((BRANCH_WIDTH,), F32), jnp.zeros((BRANCH_WIDTH,), F32)])
        wvm_up = _pad_rows(w_vmix_up[l - 1], LANES) if has_vmix else jnp.zeros((LANES, BRANCH_WIDTH), F32)
        y_rwkv, v_cur = _rwkv(
            zr, v_first, mu, par, _pad_rows(w_decay_up[l], LANES), _pad_rows(w_iclr_up[l], LANES), w_gate_up[l],
            wvm_up, jnp.stack([lnx_g[l], lnx_b[l]]), has_vmix, B, S)
        if l == 0:
            v_first = v_cur

        y_pool = _pool(zp, pool_w[l].astype(BF16), pool_scale[l][None, :], B, S)

        rw = jnp.zeros((D, LANES), F32).at[:, :N_GROUPS].set(router_grp_w[l]).at[:, N_EXPERTS:2 * N_EXPERTS].set(router_exp_w[l])
        rw_hi = rw.astype(BF16)
        rw_lo = (rw - rw_hi.astype(F32)).astype(BF16)
        rb = jnp.zeros((1, LANES), F32).at[0, :N_GROUPS].set(router_grp_b[l]).at[0, N_EXPERTS:2 * N_EXPERTS].set(router_exp_b[l])
        x1, h2, eid, wts, cnt = _merge(xf, gates, y_attn, y_rwkv, y_pool, w_branch[l].astype(BF16),
                                       w_out[l].astype(BF16), norm2_g[l][None, :], jnp.stack([rw_hi, rw_lo]), rb)

        dest, pends = _moe_dest(eid, cnt)
        dest_flat = dest.reshape(-1)
        blk_row = jnp.arange(nblk, dtype=jnp.int32)[:, None] * MOE_BLOCK
        blk_e = jnp.minimum(jnp.sum((pends[0, :N_EXPERTS][None, :] <= blk_row).astype(jnp.int32), axis=1),
                            N_EXPERTS - 1)
        buf = _moe_scatter(dest_flat, h2, jnp.zeros((R, D), F32))
        yb = _moe_experts(blk_e, buf, exp_w1[l].astype(BF16), exp_w3[l].astype(BF16), exp_w2[l].astype(BF16))
        xf = _moe_combine(dest_flat, x1, wts, final_norm_g[None, :], yb, l == DEPTH - 1)
    return xf.reshape(B, S, D)
```
